```python
import jax, jax.numpy as jnp
from jax import lax
import numpy as np

D_MODEL = 1024
BATCH = 2
SEQ = 8192
DEPTH = 4
DEC_BATCH = 128
DEC_SEQ = 1
PAST_LEN = 2048
PAGE_SIZE = 128

HEAD_DIM = 64
N_HEADS_A = 8
N_HEADS_B = 8
IDX_HEADS = 4
IDX_DIM = 64
TOPK_MAX = 256
Q_BLOCK = 128
ROPE_THETA = 500000.0
H_C = 8
HGRN_DK = 128
HGRN_DV = D_MODEL // H_C
HGRN_CHUNK = 64
D_FF = 2816
N_EXPERTS = 8
TOP_K_EXPERTS = 2
D_FF_EXPERT = 3584
PLE_DIM = 256
N_EVEN = (DEPTH + 1) // 2
N_ODD = DEPTH // 2
DEEPNORM_ALPHA = (2 * DEPTH) ** 0.25
DEEPNORM_BETA = (8 * DEPTH) ** -0.25
LN_EPS = 1e-5
RMS_EPS = 1e-6
NEG_BIG = -1e30
WIDTH_A = N_HEADS_A * HEAD_DIM
WIDTH_B = N_HEADS_B * HEAD_DIM
EVEN_IN_SPLITS = (WIDTH_A, WIDTH_A, WIDTH_A, IDX_HEADS * IDX_DIM, IDX_DIM, IDX_HEADS, WIDTH_B, WIDTH_B, WIDTH_B)
EVEN_IN_DIM = sum(EVEN_IN_SPLITS)
ROW_SPLITS = (WIDTH_A, WIDTH_A, IDX_DIM, WIDTH_B, WIDTH_B)
ROW_DIM = sum(ROW_SPLITS)
ODD_IN_SPLITS = (H_C * HGRN_DK, H_C * HGRN_DK, H_C * HGRN_DV, H_C * HGRN_DV)
ODD_IN_DIM = sum(ODD_IN_SPLITS)

kernel_name = 'hybrid_dsa_stickbreak_hgrn2_decode_step'


def _split(h, sizes):
    return jnp.split(h, np.cumsum(sizes)[:-1].tolist(), axis=-1)


def _layernorm(x, g, b):
    xf = x.astype(jnp.float32)
    mu = jnp.mean(xf, axis=-1, keepdims=True)
    var = jnp.mean(jnp.square(xf - mu), axis=-1, keepdims=True)
    return ((xf - mu) * lax.rsqrt(var + LN_EPS) * g + b).astype(x.dtype)


def _partial_rotary(x, pos):
    d = x.shape[-1]
    rd = d // 4
    half = rd // 2
    inv_freq = ROPE_THETA ** (-jnp.arange(half, dtype=jnp.float32) * 2.0 / rd)
    ang = pos.astype(jnp.float32)[:, None] * inv_freq[None, :]
    cos = jnp.cos(ang)[None, :, None, :]
    sin = jnp.sin(ang)[None, :, None, :]
    xf = x.astype(jnp.float32)
    x1, x2, rest = xf[..., :half], xf[..., half:rd], xf[..., rd:]
    out = jnp.concatenate([x1 * cos - x2 * sin, x2 * cos + x1 * sin, rest], axis=-1)
    return out.astype(x.dtype)


def _map_query_blocks(fn, q_tensors, q_pos):
    T = q_pos.shape[0]
    qb = min(Q_BLOCK, T)
    nb = -(-T // qb)
    pad = nb * qb - T

    def prep(a):
        a = jnp.pad(a, [(0, 0), (0, pad)] + [(0, 0)] * (a.ndim - 2))
        a = a.reshape((a.shape[0], nb, qb) + a.shape[2:])
        return jnp.moveaxis(a, 1, 0)

    pos = jnp.pad(q_pos, (0, pad), mode='edge').reshape(nb, qb)
    out = lax.map(fn, (tuple(prep(a) for a in q_tensors), pos))
    out = jnp.moveaxis(out, 0, 1)
    out = out.reshape((out.shape[0], nb * qb) + out.shape[3:])
    return out[:, :T]


def _dsa_attention(q, qi, wi, q_pos, k, v, ki):
    L = k.shape[1]
    n_sel = max(1, min(TOPK_MAX, L // 4))
    k_pos = jnp.arange(L, dtype=jnp.int32)

    def block(args):
        (qb, qib, wib), pb = args
        s = jnp.einsum('bqhd,bkd->bqhk', qib, ki).astype(jnp.float32) * IDX_DIM ** -0.5
        score = jnp.einsum('bqhk,bqh->bqk', jax.nn.relu(s), wib.astype(jnp.float32) * IDX_HEADS ** -0.5)
        causal = k_pos[None, :] <= pb[:, None]
        score = jnp.where(causal[None], score, NEG_BIG)
        _, sel = lax.top_k(score, n_sel)
        valid = sel <= pb[None, :, None]
        kg = jax.vmap(lambda kk, ii: kk[ii])(k, sel)
        vg = jax.vmap(lambda vv, ii: vv[ii])(v, sel)
        logits = jnp.einsum('bqhd,bqshd->bhqs', qb, kg).astype(jnp.float32) * HEAD_DIM ** -0.5
        logits = jnp.where(valid[:, None], logits, NEG_BIG)
        p = jax.nn.softmax(logits, axis=-1)
        return jnp.einsum('bhqs,bqshd->bqhd', p.astype(vg.dtype), vg)

    return _map_query_blocks(block, (q, qi, wi), q_pos)


def _stick_breaking_attention(q, q_pos, k, v):
    L = k.shape[1]
    k_pos = jnp.arange(L, dtype=jnp.int32)

    def block(args):
        (qb,), pb = args
        z = jnp.einsum('bqhd,bkhd->bhqk', qb, k).astype(jnp.float32) * HEAD_DIM ** -0.5
        before = (k_pos[None, :] < pb[:, None])[None, None]
        log_keep = jnp.where(before, jax.nn.log_sigmoid(-z), 0.0)
        later = lax.cumsum(log_keep, axis=3, reverse=True) - log_keep
        weight = jnp.where(before, jnp.exp(jax.nn.log_sigmoid(z) + later), 0.0)
        return jnp.einsum('bhqk,bkhd->bqhd', weight.astype(v.dtype), v)

    return _map_query_blocks(block, (q,), q_pos)


def _hgrn2_recurrence(q, k, log_f, v, S0):
    B, T, H = q.shape[:3]
    c = min(HGRN_CHUNK, T)
    nc = -(-T // c)
    pad = nc * c - T

    def prep(a):
        a = jnp.pad(a, ((0, 0), (0, pad), (0, 0), (0, 0)))
        return a.reshape(B, nc, c, a.shape[2], a.shape[3]).transpose(1, 0, 3, 2, 4)

    tri = jnp.tril(jnp.ones((c, c), dtype=bool))[:, :, None]

    def step(S, blk):
        qc, kc, gc, vc = blk
        G = jnp.cumsum(gc, axis=2)
        diff = G[:, :, :, None, :] - G[:, :, None, :, :]
        decay = jnp.where(tri, jnp.exp(jnp.where(tri, diff, 0.0)), 0.0)
        scores = jnp.einsum('bhtd,bhsd,bhtsd->bhts', qc, kc, decay)
        o = jnp.einsum('bhts,bhsv->bhtv', scores, vc) + jnp.einsum('bhtd,bhdv->bhtv', qc * jnp.exp(G), S)
        G_end = G[:, :, -1:, :]
        S_new = jnp.exp(G_end)[:, :, 0, :, None] * S + jnp.einsum('bhsd,bhsv->bhdv', kc * jnp.exp(G_end - G), vc)
        return S_new, o

    S_fin, o = lax.scan(step, S0, (prep(q), prep(k), prep(log_f), prep(v)))
    o = o.transpose(1, 0, 3, 2, 4).reshape(B, nc * c, H, v.shape[-1])[:, :T]
    return o, S_fin


def _even_mixer(x, w_in, w_out, past_rows):
    B, T, _ = x.shape
    P = 0 if past_rows is None else past_rows.shape[1]
    q_pos = P + jnp.arange(T, dtype=jnp.int32)
    qa, ka, va, qi, ki, wi, qb, kb, vb = _split(x @ w_in, EVEN_IN_SPLITS)
    qa = _partial_rotary(qa.reshape(B, T, N_HEADS_A, HEAD_DIM), q_pos)
    ka = _partial_rotary(ka.reshape(B, T, N_HEADS_A, HEAD_DIM), q_pos)
    qi = _partial_rotary(qi.reshape(B, T, IDX_HEADS, IDX_DIM), q_pos)
    ki = _partial_rotary(ki[:, :, None, :], q_pos)[:, :, 0, :]
    new_rows = jnp.concatenate([ka.reshape(B, T, WIDTH_A), va, ki, kb, vb], axis=-1)
    rows = new_rows if past_rows is None else jnp.concatenate([past_rows.astype(new_rows.dtype), new_rows], axis=1)
    L = rows.shape[1]
    Ka, Va, Ki, Kb, Vb = _split(rows, ROW_SPLITS)
    o_a = _dsa_attention(qa, qi, wi, q_pos,
                         Ka.reshape(B, L, N_HEADS_A, HEAD_DIM), Va.reshape(B, L, N_HEADS_A, HEAD_DIM), Ki)
    o_b = _stick_breaking_attention(qb.reshape(B, T, N_HEADS_B, HEAD_DIM), q_pos,
                                    Kb.reshape(B, L, N_HEADS_B, HEAD_DIM), Vb.reshape(B, L, N_HEADS_B, HEAD_DIM))
    merged = jnp.concatenate([o_a.reshape(B, T, WIDTH_A), o_b.reshape(B, T, WIDTH_B)], axis=-1)
    return merged @ w_out, new_rows


def _hgrn2_mixer(x, w_in, w_out, lb, norm_w, S0):
    B, T, _ = x.shape
    q, f, i, g = _split(x @ w_in, ODD_IN_SPLITS)
    q = jax.nn.silu(q.astype(jnp.float32)).reshape(B, T, H_C, HGRN_DK) * HGRN_DK ** -0.5
    fl = f.astype(jnp.float32).reshape(B, T, H_C, HGRN_DK)
    lbh = lb.astype(jnp.float32).reshape(H_C, HGRN_DK)
    forget = lbh + (1.0 - lbh) * jax.nn.sigmoid(fl)
    log_f = jnp.log(forget)
    k = (1.0 - lbh) * jax.nn.sigmoid(-fl)
    v = i.astype(jnp.float32).reshape(B, T, H_C, HGRN_DV)
    o, S = _hgrn2_recurrence(q, k, log_f, v, S0.astype(jnp.float32))
    o = o * lax.rsqrt(jnp.mean(jnp.square(o), axis=-1, keepdims=True) + RMS_EPS) * norm_w.astype(jnp.float32)
    o = o * jax.nn.silu(g.astype(jnp.float32).reshape(B, T, H_C, HGRN_DV))
    return o.reshape(B, T, H_C * HGRN_DV).astype(x.dtype) @ w_out, S


def _swiglu(x, w_in, w_out):
    a, b = jnp.split(x @ w_in, 2, axis=-1)
    return (jax.nn.silu(a) * b) @ w_out


def _moe_swiglu(x, w_router, w_in, w_out):
    logits = (x @ w_router).astype(jnp.float32)
    top_val, top_idx = lax.top_k(logits, TOP_K_EXPERTS)
    gates = jax.nn.softmax(top_val, axis=-1)
    combine = jnp.einsum('btk,btke->bte', gates, jax.nn.one_hot(top_idx, N_EXPERTS, dtype=jnp.float32))
    out = jnp.zeros_like(x)
    for e in range(N_EXPERTS):
        out = out + combine[..., e:e + 1].astype(x.dtype) * _swiglu(x, w_in[e], w_out[e])
    return out


def _run_group(x, p, past_rows, past_states, weights):
    (w_in_even, w_out_even, w_in_odd, w_out_odd, lbs, hgrn_norm_w, ln_mix_g, ln_mix_b,
     ln_ffn_g, ln_ffn_b, w_ffn_in, w_ffn_out, w_router, w_moe_in, w_moe_out, w_ple_proj, w_ple_gate) = weights
    B = x.shape[0]
    new_rows, new_states = [], []
    for i in range(DEPTH):
        j = i // 2
        if i % 2 == 0:
            mix, rows = _even_mixer(x, w_in_even[j], w_out_even[j], None if past_rows is None else past_rows[j])
            new_rows.append(rows)
        else:
            S0 = jnp.zeros((B, H_C, HGRN_DK, HGRN_DV), jnp.float32) if past_states is None else past_states[j]
            mix, S = _hgrn2_mixer(x, w_in_odd[j], w_out_odd[j], lbs[j], hgrn_norm_w[j], S0)
            new_states.append(S)
        x = _layernorm(DEEPNORM_ALPHA * x + mix, ln_mix_g[i], ln_mix_b[i])
        if i % 2 == 0:
            ffn = _swiglu(x, w_ffn_in[j], w_ffn_out[j])
        else:
            ffn = _moe_swiglu(x, w_router[j], w_moe_in[j], w_moe_out[j])
        x = _layernorm(DEEPNORM_ALPHA * x + ffn, ln_ffn_g[i], ln_ffn_b[i])
        x = x + jax.nn.sigmoid(x @ w_ple_gate[i]) * (p[i] @ w_ple_proj[i])
    return x, jnp.stack(new_rows), jnp.stack(new_states)


def _normal(key, shape, scale):
    return jax.random.normal(key, shape, jnp.float32) * scale


def setup_inputs(seed: int = 0) -> dict:
    key = jax.random.key(seed)
    ks = jax.random.split(key, 26)
    n_pages = PAST_LEN // PAGE_SIZE
    n_used = DEC_BATCH * n_pages
    n_pool = n_used + max(1, n_used // 4)
    page_table = jax.random.permutation(ks[3], n_pool)[:n_used].reshape(DEC_BATCH, n_pages).astype(jnp.int32)
    return {
        'x_prompt': _normal(ks[0], (BATCH, SEQ, D_MODEL), 1.0),
        'x_sample': _normal(ks[1], (DEC_BATCH, DEC_SEQ, D_MODEL), 1.0),
        'cache_attn': _normal(ks[2], (N_EVEN, n_pool, PAGE_SIZE, ROW_DIM), 1.0),
        'state_hgrn': _normal(ks[4], (N_ODD, DEC_BATCH, H_C, HGRN_DK, HGRN_DV), 0.5),
        'page_table': page_table,
        'p_prompt': _normal(ks[5], (DEPTH, BATCH, SEQ, PLE_DIM), 1.0),
        'p_sample': _normal(ks[6], (DEPTH, DEC_BATCH, DEC_SEQ, PLE_DIM), 1.0),
        'w_in_even': _normal(ks[7], (N_EVEN, D_MODEL, EVEN_IN_DIM), D_MODEL ** -0.5),
        'w_out_even': _normal(ks[8], (N_EVEN, WIDTH_A + WIDTH_B, D_MODEL), DEEPNORM_BETA * (WIDTH_A + WIDTH_B) ** -0.5),
        'w_in_odd': _normal(ks[9], (N_ODD, D_MODEL, ODD_IN_DIM), D_MODEL ** -0.5),
        'w_out_odd': _normal(ks[10], (N_ODD, H_C * HGRN_DV, D_MODEL), DEEPNORM_BETA * (H_C * HGRN_DV) ** -0.5),
        'hgrn_lower_bounds': _normal(ks[11], (N_ODD, H_C * HGRN_DK), 0.5),
        'hgrn_norm_w': 1.0 + _normal(ks[12], (N_ODD, HGRN_DV), 0.05),
        'ln_mix_g': 1.0 + _normal(ks[13], (DEPTH, D_MODEL), 0.05),
        'ln_mix_b': _normal(ks[14], (DEPTH, D_MODEL), 0.02),
        'ln_ffn_g': 1.0 + _normal(ks[15], (DEPTH, D_MODEL), 0.05),
        'ln_ffn_b': _normal(ks[16], (DEPTH, D_MODEL), 0.02),
        'w_ffn_in': _normal(ks[17], (N_EVEN, D_MODEL, 2 * D_FF), D_MODEL ** -0.5),
        'w_ffn_out': _normal(ks[18], (N_EVEN, D_FF, D_MODEL), DEEPNORM_BETA * D_FF ** -0.5),
        'w_router': _normal(ks[19], (N_ODD, D_MODEL, N_EXPERTS), D_MODEL ** -0.5),
        'w_moe_in': _normal(ks[20], (N_ODD, N_EXPERTS, D_MODEL, 2 * D_FF_EXPERT), D_MODEL ** -0.5),
        'w_moe_out': _normal(ks[21], (N_ODD, N_EXPERTS, D_FF_EXPERT, D_MODEL), DEEPNORM_BETA * D_FF_EXPERT ** -0.5),
        'w_ple_proj': _normal(ks[22], (DEPTH, PLE_DIM, D_MODEL), PLE_DIM ** -0.5),
        'w_ple_gate': _normal(ks[23], (DEPTH, D_MODEL, D_MODEL), D_MODEL ** -0.5),
    }


def reference(x_prompt, x_sample, cache_attn, state_hgrn, page_table, p_prompt, p_sample,
              w_in_even, w_out_even, w_in_odd, w_out_odd, hgrn_lower_bounds, hgrn_norm_w,
              ln_mix_g, ln_mix_b, ln_ffn_g, ln_ffn_b, w_ffn_in, w_ffn_out,
              w_router, w_moe_in, w_moe_out, w_ple_proj, w_ple_gate):
    lbs = jax.nn.softmax(hgrn_lower_bounds.astype(jnp.float32), axis=0)
    lbs = jnp.cumsum(lbs, axis=0) - lbs[0:1]
    weights = (w_in_even, w_out_even, w_in_odd, w_out_odd, lbs, hgrn_norm_w, ln_mix_g, ln_mix_b,
               ln_ffn_g, ln_ffn_b, w_ffn_in, w_ffn_out, w_router, w_moe_in, w_moe_out, w_ple_proj, w_ple_gate)
    y_prompt, rows_prompt, hgrn_prompt = _run_group(x_prompt, p_prompt, None, None, weights)
    n_seq = x_sample.shape[0]
    past_rows = [cache_attn[j][page_table].reshape(n_seq, -1, ROW_DIM) for j in range(N_EVEN)]
    y_sample, rows_sample, hgrn_sample = _run_group(x_sample, p_sample, past_rows, state_hgrn, weights)
    return (y_prompt, y_sample, rows_prompt, rows_sample, hgrn_prompt, hgrn_sample)
```

```python
import functools

import jax
import jax.numpy as jnp
import numpy as np
from jax import lax
from jax.experimental import pallas as pl
from jax.experimental.pallas import tpu as pltpu

D_MODEL = 1024
DEPTH = 4
PAGE_SIZE = 128
HEAD_DIM = 64
N_HEADS_A = 8
N_HEADS_B = 8
IDX_HEADS = 4
IDX_DIM = 64
TOPK_MAX = 256
ROPE_THETA = 500000.0
H_C = 8
HGRN_DK = 128
HGRN_DV = D_MODEL // H_C
D_FF = 2816
N_EXPERTS = 8
D_FF_EXPERT = 3584
PLE_DIM = 256
DEEPNORM_ALPHA = (2 * DEPTH) ** 0.25
LN_EPS = 1e-5
RMS_EPS = 1e-6
NEG_BIG = -1e30
WIDTH_A = N_HEADS_A * HEAD_DIM
WIDTH_B = N_HEADS_B * HEAD_DIM
ROW_DIM = 2 * WIDTH_A + IDX_DIM + 2 * WIDTH_B

LANES = 128
HALF = LANES // 2
VMEM_LIMIT = 56 * 1024 * 1024
TOKEN_TILE = 384
INT_MIN = -2 ** 31
NO_INDEX = 2 ** 30

BF16 = jnp.bfloat16
F32 = jnp.float32


def _cparams(*sem, vmem=VMEM_LIMIT):
    return pltpu.CompilerParams(dimension_semantics=sem, vmem_limit_bytes=vmem)


def _dot(a, b):
    return jnp.dot(a, b, preferred_element_type=F32)


def _dot_nt(a, b):
    return lax.dot_general(a, b, (((1,), (1,)), ((), ())), preferred_element_type=F32)


def _dot_tn(a, b):
    return lax.dot_general(a, b, (((0,), (0,)), ((), ())), preferred_element_type=F32)


def _low_half(shape):
    return lax.broadcasted_iota(jnp.int32, shape, len(shape) - 1) < HALF


def _split_hi_lo(x):
    hi = x.astype(BF16)
    lo = (x - hi.astype(F32)).astype(BF16)
    return hi, lo


def _rotary(t, c, s1, s2):
    return t * c + pltpu.roll(t, LANES - 8, 1) * s1 + pltpu.roll(t, 8, 1) * s2


def _even_inproj_kernel(x_ref, wr_ref, wq_ref, cs_ref, rows_ref, qa_ref, qi_ref, wi_ref, qb_ref,
                        ka_ref, va_ref, ki_ref, kb_ref, vb_ref):
    xb = x_ref[...].astype(BF16)
    c = cs_ref[:, 0:LANES]
    s1 = cs_ref[:, LANES:2 * LANES]
    s2 = cs_ref[:, 2 * LANES:3 * LANES]
    low = _low_half(c.shape)
    acc = _dot(xb, wr_ref[...])
    for t in range(WIDTH_A // LANES):
        sl = slice(t * LANES, (t + 1) * LANES)
        r = _rotary(acc[:, sl], c, s1, s2)
        rows_ref[:, sl] = r
        ka_ref[:, sl] = r.astype(BF16)
    v_a = acc[:, WIDTH_A:2 * WIDTH_A]
    rows_ref[:, WIDTH_A:2 * WIDTH_A] = v_a
    va_ref[...] = v_a.astype(BF16)
    o_i = 2 * WIDTH_A
    t8 = acc[:, o_i:o_i + LANES]
    r8 = _rotary(t8, jnp.where(low, c, 1.0), jnp.where(low, s1, 0.0), jnp.where(low, s2, 0.0))
    rows_ref[:, o_i:o_i + LANES] = r8
    ki_ref[...] = jnp.where(low, r8, pltpu.roll(r8, HALF, 1)).astype(BF16)
    rows_ref[:, o_i + LANES:ROW_DIM] = acc[:, o_i + LANES:ROW_DIM]
    o_b = o_i + IDX_DIM
    kb_ref[...] = acc[:, o_b:o_b + WIDTH_B].astype(BF16)
    vb_ref[...] = acc[:, o_b + WIDTH_B:o_b + 2 * WIDTH_B].astype(BF16)

    accq = _dot(xb, wq_ref[...])
    scale = HEAD_DIM ** -0.5

    def put_pairs(dst_ref, src_off, n_pairs, rot):
        for p in range(n_pairs):
            t = accq[:, src_off + p * LANES:src_off + (p + 1) * LANES]
            if rot:
                t = _rotary(t, c, s1, s2)
            t = t * scale
            dst_ref[:, (2 * p) * LANES:(2 * p + 1) * LANES] = jnp.where(low, t, 0.0).astype(BF16)
            dst_ref[:, (2 * p + 1) * LANES:(2 * p + 2) * LANES] = jnp.where(low, 0.0, t).astype(BF16)

    put_pairs(qa_ref, 0, N_HEADS_A // 2, True)
    put_pairs(qi_ref, WIDTH_A, IDX_HEADS // 2, True)
    put_pairs(qb_ref, WIDTH_A + IDX_HEADS * IDX_DIM, N_HEADS_B // 2, False)
    o_w = WIDTH_A + IDX_HEADS * IDX_DIM + WIDTH_B
    wi_ref[...] = accq[:, o_w:o_w + LANES] * IDX_HEADS ** -0.5


def _even_inproj(x, w_rows, w_q, cs, tm=TOKEN_TILE):
    n = x.shape[0]
    qw = w_q.shape[1]
    row = lambda i: (i, 0)
    const = lambda i: (0, 0)
    outs = [
        (ROW_DIM, F32), (2 * WIDTH_A, BF16), (2 * IDX_HEADS * IDX_DIM, BF16), (LANES, F32), (2 * WIDTH_B, BF16),
        (WIDTH_A, BF16), (WIDTH_A, BF16), (LANES, BF16), (WIDTH_B, BF16), (WIDTH_B, BF16),
    ]
    return pl.pallas_call(
        _even_inproj_kernel,
        grid=(pl.cdiv(n, tm),),
        in_specs=[pl.BlockSpec((tm, D_MODEL), row), pl.BlockSpec((D_MODEL, ROW_DIM), const),
                  pl.BlockSpec((D_MODEL, qw), const), pl.BlockSpec((tm, 3 * LANES), row)],
        out_specs=[pl.BlockSpec((tm, w), row) for w, _ in outs],
        out_shape=[jax.ShapeDtypeStruct((n, w), d) for w, d in outs],
        compiler_params=_cparams("parallel"),
        name="even_inproj",
    )(x, w_rows, w_q, cs)


def _matmul_kernel(x_ref, w_ref, o_ref):
    o_ref[...] = _dot(x_ref[...].astype(BF16), w_ref[...]).astype(o_ref.dtype)


def _matmul(x, w, out_dtype=F32, tm=TOKEN_TILE, tn=1024):
    n, k = x.shape
    m = w.shape[1]
    tn = min(tn, m)
    return pl.pallas_call(
        _matmul_kernel,
        grid=(pl.cdiv(n, tm), m // tn),
        in_specs=[pl.BlockSpec((tm, k), lambda i, j: (i, 0)), pl.BlockSpec((k, tn), lambda i, j: (0, j))],
        out_specs=pl.BlockSpec((tm, tn), lambda i, j: (i, j)),
        out_shape=jax.ShapeDtypeStruct((n, m), out_dtype),
        compiler_params=_cparams("parallel", "arbitrary"),
        name="matmul",
    )(x, w)


def _swiglu_in_kernel(x_ref, wa_ref, wb_ref, o_ref):
    xb = x_ref[...].astype(BF16)
    a = _dot(xb, wa_ref[...])
    b = _dot(xb, wb_ref[...])
    o_ref[...] = (a * jax.nn.sigmoid(a) * b).astype(o_ref.dtype)


def _swiglu_in(x, w_in, tm=TOKEN_TILE, tn=1408):
    n, k = x.shape
    f = w_in.shape[1] // 2
    nj = f // tn
    return pl.pallas_call(
        _swiglu_in_kernel,
        grid=(pl.cdiv(n, tm), nj),
        in_specs=[pl.BlockSpec((tm, k), lambda i, j: (i, 0)),
                  pl.BlockSpec((k, tn), lambda i, j: (0, j)),
                  pl.BlockSpec((k, tn), lambda i, j: (0, j + nj))],
        out_specs=pl.BlockSpec((tm, tn), lambda i, j: (i, j)),
        out_shape=jax.ShapeDtypeStruct((n, f), BF16),
        compiler_params=_cparams("parallel", "arbitrary"),
        name="swiglu_in",
    )(x, w_in, w_in)


def _layernorm(y, g, b):
    mu = jnp.mean(y, axis=-1, keepdims=True)
    d = y - mu
    var = jnp.mean(d * d, axis=-1, keepdims=True)
    return d * lax.rsqrt(var + LN_EPS) * g + b


def _mix_ln_kernel(n_mm, n_comb, ple, *refs):
    it = iter(refs)
    mm = [(next(it), next(it)) for _ in range(n_mm)]
    comb = [next(it) for _ in range(n_comb)]
    gates_ref = next(it) if n_comb else None
    x_ref, g_ref, b_ref = next(it), next(it), next(it)
    if ple:
        p_ref, wg_ref, wp_ref = next(it), next(it), next(it)
    o_ref = next(it)
    y = DEEPNORM_ALPHA * x_ref[...]
    for h_ref, w_ref in mm:
        y = y + _dot(h_ref[...], w_ref[...])
    for k, y_ref in enumerate(comb):
        y = y + gates_ref[:, k:k + 1] * y_ref[...]
    y = _layernorm(y, g_ref[...], b_ref[...])
    if ple:
        gate = jax.nn.sigmoid(_dot(y.astype(BF16), wg_ref[...]))
        y = y + gate * _dot(p_ref[...].astype(BF16), wp_ref[...])
    o_ref[...] = y


def _mix_ln(x, g, b, mm=(), comb=(), gates=None, ple=None, tm=TOKEN_TILE):
    n = x.shape[0]
    row = lambda i: (i, 0)
    const = lambda i: (0, 0)
    args, specs = [], []
    for h, w in mm:
        args += [h, w]
        specs += [pl.BlockSpec((tm, h.shape[1]), row), pl.BlockSpec(w.shape, const)]
    for y in comb:
        args.append(y)
        specs.append(pl.BlockSpec((tm, D_MODEL), row))
    if comb:
        args.append(gates)
        specs.append(pl.BlockSpec((tm, gates.shape[1]), row))
    args += [x, g.reshape(1, D_MODEL), b.reshape(1, D_MODEL)]
    specs += [pl.BlockSpec((tm, D_MODEL), row), pl.BlockSpec((1, D_MODEL), const), pl.BlockSpec((1, D_MODEL), const)]
    if ple is not None:
        p, wg, wp = ple
        args += [p, wg, wp]
        specs += [pl.BlockSpec((tm, PLE_DIM), row), pl.BlockSpec(wg.shape, const), pl.BlockSpec(wp.shape, const)]
    return pl.pallas_call(
        functools.partial(_mix_ln_kernel, len(mm), len(comb), ple is not None),
        grid=(pl.cdiv(n, tm),),
        in_specs=specs,
        out_specs=pl.BlockSpec((tm, D_MODEL), row),
        out_shape=jax.ShapeDtypeStruct((n, D_MODEL), F32),
        compiler_params=_cparams("parallel"),
        name="mix_ln",
    )(*args)


def _prep_even_in(w):
    sizes = (WIDTH_A, WIDTH_A, WIDTH_A, IDX_HEADS * IDX_DIM, IDX_DIM, IDX_HEADS, WIDTH_B, WIDTH_B, WIDTH_B)
    qa, ka, va, qi, ki, wi, qb, kb, vb = jnp.split(w, np.cumsum(sizes)[:-1].tolist(), axis=1)
    w_rows = jnp.concatenate([ka, va, ki, kb, vb], axis=1)
    pad = jnp.zeros((w.shape[0], LANES - IDX_HEADS), w.dtype)
    w_q = jnp.concatenate([qa, qi, qb, wi, pad], axis=1)
    return w_rows.astype(BF16), w_q.astype(BF16)


def _rotary_table(pos):
    rd = HEAD_DIM // 4
    half = rd // 2
    inv_freq = ROPE_THETA ** (-jnp.arange(half, dtype=F32) * 2.0 / rd)
    ang = pos.astype(F32)[:, None] * inv_freq[None, :]
    cos, sin = jnp.cos(ang), jnp.sin(ang)
    n = pos.shape[0]
    ones = jnp.ones((n, HEAD_DIM - rd), F32)
    zeros = jnp.zeros((n, HEAD_DIM - rd), F32)
    zh = jnp.zeros((n, half), F32)
    c = jnp.concatenate([cos, cos, ones], axis=1)
    s1 = jnp.concatenate([-sin, zh, zeros], axis=1)
    s2 = jnp.concatenate([zh, sin, zeros], axis=1)
    return jnp.concatenate([c, c, s1, s1, s2, s2], axis=1)


def _softplus(z):
    return jnp.maximum(z, 0.0) + jnp.log1p(jnp.exp(-jnp.abs(z)))


def _sb_tile(q, kt, vt, upper, carry, acc, valid):
    z = _dot_nt(q, kt)
    sp = _softplus(z)
    lk = -sp if valid is None else jnp.where(valid, -sp, 0.0)
    hi, lo = _split_hi_lo(lk)
    later = _dot(hi, upper) + _dot(lo, upper) + carry
    w = jnp.exp(z - sp + later)
    if valid is not None:
        w = jnp.where(valid, w, 0.0)
    acc = acc + _dot(w.astype(BF16), vt)
    carry = carry + jnp.sum(lk, axis=1, keepdims=True)
    return carry, acc


def _sb_prompt_kernel(q_ref, k_ref, v_ref, o_ref, *, tq):
    qi = pl.program_id(2)
    rows = lax.broadcasted_iota(jnp.int32, (tq, tq), 0)
    cols = lax.broadcasted_iota(jnp.int32, (tq, tq), 1)
    upper = (rows > cols).astype(BF16)
    diag_valid = cols < rows
    qs = (q_ref[:, 0:LANES], q_ref[:, LANES:2 * LANES])

    def tile(j, state, valid):
        start = pl.multiple_of(j * tq, tq)
        kt = k_ref[pl.ds(start, tq), :]
        vt = v_ref[pl.ds(start, tq), :]
        return tuple(_sb_tile(q, kt, vt, upper, carry, acc, valid) for q, (carry, acc) in zip(qs, state))

    zero = (jnp.zeros((tq, 1), F32), jnp.zeros((tq, LANES), F32))
    state = tile(qi, (zero, zero), diag_valid)
    state = lax.fori_loop(0, qi, lambda jj, st: tile(qi - 1 - jj, st, None), state)
    low = _low_half((tq, LANES))
    o_ref[...] = jnp.where(low, state[0][1], state[1][1]).astype(o_ref.dtype)


def _sb_prompt(q_pad, k, v, batch, t, tq=256):
    tq = min(tq, t)
    nq = t // tq
    return pl.pallas_call(
        functools.partial(_sb_prompt_kernel, tq=tq),
        grid=(batch, N_HEADS_B // 2, nq),
        in_specs=[pl.BlockSpec((tq, 2 * LANES), lambda b, p, i: (b * nq + i, p)),
                  pl.BlockSpec((t, LANES), lambda b, p, i: (b, p)),
                  pl.BlockSpec((t, LANES), lambda b, p, i: (b, p))],
        out_specs=pl.BlockSpec((tq, LANES), lambda b, p, i: (b * nq + i, p)),
        out_shape=jax.ShapeDtypeStruct((batch * t, WIDTH_B), BF16),
        compiler_params=_cparams("parallel", "parallel", "arbitrary"),
        name="sb_prompt",
    )(q_pad, k, v)


def _sortable(x):
    b = lax.bitcast_convert_type(x + 0.0, jnp.int32)
    return b ^ ((b >> 31) & 0x7FFFFFFF)


def _dsa_prompt_kernel(qa_ref, qi_ref, wi_ref, ka_ref, va_ref, ki_ref, o_ref, key_sc, m_sc, l_sc, acc_sc,
                       *, tq, tk, n_sel, t):
    q0 = pl.program_id(1) * tq
    nkt = (q0 + tq + tk - 1) // tk
    rows = q0 + lax.broadcasted_iota(jnp.int32, (tq, tk), 0)
    cols0 = lax.broadcasted_iota(jnp.int32, (tq, tk), 1)
    w = wi_ref[...]

    def score_tile(j, carry):
        start = pl.multiple_of(j * tk, tk)
        kt = ki_ref[pl.ds(start, tk), :]
        score = jnp.zeros((tq, tk), F32)
        for h in range(IDX_HEADS):
            s = _dot_nt(qi_ref[:, h * LANES:(h + 1) * LANES], kt)
            score = score + jnp.maximum(s, 0.0) * w[:, h:h + 1]
        score = jnp.where(start + cols0 <= rows, score, NEG_BIG)
        key_sc[:, pl.ds(start, tk)] = _sortable(score)
        return carry

    lax.fori_loop(0, nkt, score_tile, 0)

    def count(pred):
        def body(j, c):
            start = pl.multiple_of(j * tk, tk)
            return c + jnp.where(pred(key_sc[:, pl.ds(start, tk)], start + cols0), 1, 0)
        c = lax.fori_loop(0, nkt, body, jnp.zeros((tq, tk), jnp.int32))
        return jnp.sum(c, axis=1, keepdims=True)

    def bit_step(i, v):
        cand = v ^ lax.shift_left(jnp.int32(1), 31 - i)
        return jnp.where(count(lambda kk, kp: kk >= cand) >= n_sel, cand, v)

    v = lax.fori_loop(0, 32, bit_step, jnp.full((tq, 1), INT_MIN, jnp.int32))
    r = n_sel - count(lambda kk, kp: kk > v)
    n_eq = count(lambda kk, kp: kk == v)

    def tie_cut():
        nbits = (t - 1).bit_length()

        def step(i, c):
            cand = c + lax.shift_left(jnp.int32(1), nbits - 1 - i)
            g = count(lambda kk, kp: jnp.where(kk == v, kp, NO_INDEX) < cand)
            return jnp.where(g < r, cand, c)

        return lax.fori_loop(0, nbits, step, jnp.zeros((tq, 1), jnp.int32))

    cut = lax.cond(jnp.max(n_eq - r) > 0, tie_cut, lambda: jnp.full((tq, 1), t, jnp.int32))

    m_sc[...] = jnp.full(m_sc.shape, NEG_BIG, F32)
    l_sc[...] = jnp.zeros(l_sc.shape, F32)
    acc_sc[...] = jnp.zeros(acc_sc.shape, F32)

    def attn_tile(j, carry):
        start = pl.multiple_of(j * tk, tk)
        kk = key_sc[:, pl.ds(start, tk)]
        kp = start + cols0
        tie_ok = jnp.where(kk == v, kp, NO_INDEX) <= cut
        bias = jnp.where(kp <= rows, jnp.where(kk > v, 0.0, jnp.where(tie_ok, 0.0, NEG_BIG)), NEG_BIG)
        for p in range(N_HEADS_A // 2):
            kt = ka_ref[pl.ds(start, tk), p * LANES:(p + 1) * LANES]
            vt = va_ref[pl.ds(start, tk), p * LANES:(p + 1) * LANES]
            for h in (2 * p, 2 * p + 1):
                s = _dot_nt(qa_ref[:, h * LANES:(h + 1) * LANES], kt) + bias
                m_old = m_sc[h]
                m_new = jnp.maximum(m_old, jnp.max(s, axis=1, keepdims=True))
                alpha = jnp.exp(m_old - m_new)
                pr = jnp.exp(s - m_new)
                l_sc[h] = alpha * l_sc[h] + jnp.sum(pr, axis=1, keepdims=True)
                acc_sc[h] = alpha * acc_sc[h] + _dot(pr.astype(BF16), vt)
                m_sc[h] = m_new
        return carry

    lax.fori_loop(0, nkt, attn_tile, 0)
    low = _low_half((tq, LANES))
    for p in range(N_HEADS_A // 2):
        o_even = acc_sc[2 * p] / l_sc[2 * p]
        o_odd = acc_sc[2 * p + 1] / l_sc[2 * p + 1]
        o_ref[:, p * LANES:(p + 1) * LANES] = jnp.where(low, o_even, o_odd).astype(o_ref.dtype)


def _dsa_prompt(qa_pad, qi_pad, wi, ka, va, ki2, batch, t, tq=128, tk=256):
    tq = min(tq, t)
    tk = min(tk, t)
    nq = t // tq
    n_sel = max(1, min(TOPK_MAX, t // 4))
    qrow = lambda b, i: (b * nq + i, 0)
    seq = lambda b, i: (b, 0)
    return pl.pallas_call(
        functools.partial(_dsa_prompt_kernel, tq=tq, tk=tk, n_sel=n_sel, t=t),
        grid=(batch, nq),
        in_specs=[pl.BlockSpec((tq, 2 * WIDTH_A), qrow), pl.BlockSpec((tq, 2 * IDX_HEADS * IDX_DIM), qrow),
                  pl.BlockSpec((tq, LANES), qrow), pl.BlockSpec((t, WIDTH_A), seq), pl.BlockSpec((t, WIDTH_A), seq),
                  pl.BlockSpec((t, LANES), seq)],
        out_specs=pl.BlockSpec((tq, WIDTH_A), qrow),
        out_shape=jax.ShapeDtypeStruct((batch * t, WIDTH_A), BF16),
        scratch_shapes=[pltpu.VMEM((tq, t), jnp.int32), pltpu.VMEM((N_HEADS_A, tq, 1), F32),
                        pltpu.VMEM((N_HEADS_A, tq, 1), F32), pltpu.VMEM((N_HEADS_A, tq, LANES), F32)],
        compiler_params=_cparams("parallel", "arbitrary"),
        name="dsa_prompt",
    )(qa_pad, qi_pad, wi, ka, va, ki2)


OFF_KI = 2 * WIDTH_A
OFF_KB = OFF_KI + IDX_DIM
OFF_VB = OFF_KB + WIDTH_B
VB_WIN = (OFF_VB // LANES) * LANES


def _head_rows(row, off, n_rows=8):
    w = row.shape[1]
    col = lax.broadcasted_iota(jnp.int32, (n_rows, w), 1)
    hrow = lax.broadcasted_iota(jnp.int32, (n_rows, w), 0)
    head_of_col = ((col - off + HEAD_DIM) >> 6) - 1
    return jnp.where(head_of_col == hrow, jnp.broadcast_to(row, (n_rows, w)), 0.0)


def _sum_all(x):
    return jnp.sum(jnp.sum(x, axis=0, keepdims=True), axis=1, keepdims=True)


def _even_decode_kernel(pt_ref, qa_ref, qi_ref, wi_ref, qb_ref, self_ref, *rest, n_pages, n_sel):
    pages = rest[:n_pages]
    oa_ref, ob_ref, key_sc, la_sc, z_sc = rest[n_pages:]
    n_past = n_pages * PAGE_SIZE
    f32_tile = lambda ref, h: ref[:, h * LANES:(h + 1) * LANES].astype(F32)
    swap = lambda t: pltpu.roll(t, HALF, 1)

    qa_row = jnp.concatenate([f32_tile(qa_ref, 2 * p) + f32_tile(qa_ref, 2 * p + 1)
                              for p in range(N_HEADS_A // 2)], axis=1)
    qa_bd = _head_rows(qa_row, 0)
    qa_bd16 = qa_bd.astype(BF16)
    zero_tile = jnp.zeros((1, LANES), F32)
    even_b = [swap(f32_tile(qb_ref, 2 * p)) for p in range(N_HEADS_B // 2)] + [zero_tile]
    odd_b = [zero_tile] + [swap(f32_tile(qb_ref, 2 * p + 1)) for p in range(N_HEADS_B // 2)]
    qb_row = jnp.concatenate([e + o for e, o in zip(even_b, odd_b)], axis=1)
    qb_bd16 = _head_rows(qb_row, OFF_KB - OFF_KI).astype(BF16)
    kb_win = qb_row.shape[1]
    hrow = lax.broadcasted_iota(jnp.int32, (8, LANES), 0)
    lane8 = lax.broadcasted_iota(jnp.int32, (8, LANES), 1)
    qi_rows = jnp.zeros((8, LANES), F32)
    for h in range(IDX_HEADS):
        t = f32_tile(qi_ref, h)
        t = swap(t) if h % 2 else t
        qi_rows = jnp.where(hrow == h, jnp.broadcast_to(t, (8, LANES)), qi_rows)
    qi_rows16 = qi_rows.astype(BF16)
    w_col = jnp.sum(jnp.where(lane8 == hrow, jnp.broadcast_to(wi_ref[...], (8, LANES)), 0.0), axis=1, keepdims=True)

    def idx_score(s_i):
        return jnp.sum(jnp.maximum(s_i, 0.0) * w_col, axis=0, keepdims=True)

    for s in range(n_pages):
        pg = pages[s]
        s_i = _dot_nt(qi_rows16, pg[:, OFF_KI:OFF_KI + LANES].astype(BF16))
        key_sc[s:s + 1, :] = _sortable(idx_score(s_i))
        la_sc[s] = _dot_nt(qa_bd16, pg[:, 0:WIDTH_A].astype(BF16))
        z_sc[s] = _dot_nt(qb_bd16, pg[:, OFF_KI:OFF_KI + kb_win].astype(BF16))

    rnd = lambda x: x.astype(BF16).astype(F32)
    s_self = jnp.sum(qi_rows16.astype(F32) * rnd(self_ref[:, OFF_KI:OFF_KI + LANES]), axis=1, keepdims=True)
    sc_self = idx_score(s_self)
    la_self = jnp.sum(qa_bd16.astype(F32) * rnd(self_ref[:, 0:WIDTH_A]), axis=1, keepdims=True)
    lane1 = lax.broadcasted_iota(jnp.int32, (1, LANES), 1)
    key_sc[n_pages:n_pages + 1, :] = jnp.where(lane1 == 0, _sortable(jnp.broadcast_to(sc_self, (1, LANES))), INT_MIN)

    keys = key_sc[...]
    kp = (lax.broadcasted_iota(jnp.int32, keys.shape, 0) * PAGE_SIZE
          + lax.broadcasted_iota(jnp.int32, keys.shape, 1))
    count = lambda pred: _sum_all(jnp.where(pred, 1, 0))

    def bit_step(i, v):
        cand = v ^ lax.shift_left(jnp.int32(1), 31 - i)
        return jnp.where(count(keys >= cand) >= n_sel, cand, v)

    v = lax.fori_loop(0, 32, bit_step, jnp.full((1, 1), INT_MIN, jnp.int32))
    r = n_sel - count(keys > v)
    n_eq = count(keys == v)
    tie_pos = jnp.where(keys == v, kp, NO_INDEX)

    def tie_cut():
        nbits = n_past.bit_length()

        def step(i, c):
            cand = c + lax.shift_left(jnp.int32(1), nbits - 1 - i)
            return jnp.where(count(tie_pos < cand) < r, cand, c)

        return lax.fori_loop(0, nbits, step, jnp.zeros((1, 1), jnp.int32))

    cut = lax.cond(jnp.max(n_eq - r) > 0, tie_cut, lambda: jnp.full((1, 1), NO_INDEX - 1, jnp.int32))
    bias = jnp.where(kp <= n_past, jnp.where(keys > v, 0.0, jnp.where(tie_pos <= cut, 0.0, NEG_BIG)), NEG_BIG)

    bias_self = bias[n_pages:n_pages + 1, 0:1]
    m = la_self + bias_self
    for s in range(n_pages):
        m = jnp.maximum(m, jnp.max(la_sc[s] + bias[s:s + 1, :], axis=1, keepdims=True))
    p_self = jnp.exp(la_self + bias_self - m)
    l = p_self
    acc = rnd(p_self) * rnd(self_ref[:, WIDTH_A:2 * WIDTH_A])
    for s in range(n_pages):
        pr = jnp.exp(la_sc[s] + bias[s:s + 1, :] - m)
        l = l + jnp.sum(pr, axis=1, keepdims=True)
        acc = acc + _dot(pr.astype(BF16), pages[s][:, WIDTH_A:2 * WIDTH_A].astype(BF16))
    oa_ref[...] = jnp.sum(_head_rows(jnp.ones((1, WIDTH_A), F32), 0) * (acc / l), axis=0,
                          keepdims=True).astype(oa_ref.dtype)

    t_idx = lax.broadcasted_iota(jnp.int32, (PAGE_SIZE, PAGE_SIZE), 0)
    s_idx = lax.broadcasted_iota(jnp.int32, (PAGE_SIZE, PAGE_SIZE), 1)
    upper = (t_idx > s_idx).astype(BF16)
    vb_win = ROW_DIM - VB_WIN
    carry = jnp.zeros((8, 1), F32)
    acc_b = jnp.zeros((8, vb_win), F32)
    for s in reversed(range(n_pages)):
        z = z_sc[s]
        sp = _softplus(z)
        hi, lo = _split_hi_lo(-sp)
        later = _dot(hi, upper) + _dot(lo, upper) + carry
        w = jnp.exp(z - sp + later)
        acc_b = acc_b + _dot(w.astype(BF16), pages[s][:, VB_WIN:ROW_DIM].astype(BF16))
        carry = carry - jnp.sum(sp, axis=1, keepdims=True)
    picked = jnp.sum(_head_rows(jnp.ones((1, vb_win), F32), OFF_VB - VB_WIN) * acc_b, axis=0, keepdims=True)
    ob_ref[...] = picked[:, OFF_VB - VB_WIN:OFF_VB - VB_WIN + WIDTH_B].astype(ob_ref.dtype)


def _even_decode(page_table, cache, qa_pad, qi_pad, wi, qb_pad, self_rows):
    n, n_pages = page_table.shape
    n_sel = max(1, min(TOPK_MAX, (n_pages * PAGE_SIZE + 1) // 4))
    per_seq = lambda w: pl.BlockSpec((None, 1, w), lambda b, pt: (b, 0, 0))
    page_spec = lambda s: pl.BlockSpec((None, PAGE_SIZE, ROW_DIM), lambda b, pt: (pt[b, s], 0, 0))
    r3 = lambda a: a.reshape(n, 1, a.shape[-1])
    oa, ob = pl.pallas_call(
        functools.partial(_even_decode_kernel, n_pages=n_pages, n_sel=n_sel),
        grid_spec=pltpu.PrefetchScalarGridSpec(
            num_scalar_prefetch=1, grid=(n,),
            in_specs=[per_seq(2 * WIDTH_A), per_seq(2 * IDX_HEADS * IDX_DIM), per_seq(LANES), per_seq(2 * WIDTH_B),
                      per_seq(ROW_DIM)] + [page_spec(s) for s in range(n_pages)],
            out_specs=[per_seq(WIDTH_A), per_seq(WIDTH_B)],
            scratch_shapes=[pltpu.VMEM((n_pages + 1, LANES), jnp.int32), pltpu.VMEM((n_pages, 8, LANES), F32),
                            pltpu.VMEM((n_pages, 8, LANES), F32)]),
        out_shape=[jax.ShapeDtypeStruct((n, 1, WIDTH_A), BF16), jax.ShapeDtypeStruct((n, 1, WIDTH_B), BF16)],
        compiler_params=_cparams("parallel"),
        name="even_decode",
    )(page_table, r3(qa_pad), r3(qi_pad), r3(wi), r3(qb_pad), r3(self_rows), *([cache] * n_pages))
    return oa.reshape(n, WIDTH_A), ob.reshape(n, WIDTH_B)


def _hgrn_lower_bound(lb_ref, layer):
    x = lb_ref[...]
    e = jnp.exp(x - jnp.max(x, axis=0, keepdims=True))
    sm = e / jnp.sum(e, axis=0, keepdims=True)
    lb = jnp.zeros((1, sm.shape[1]), F32)
    for r in range(1, layer + 1):
        lb = lb + sm[r:r + 1, :]
    return lb


def _hgrn_gates(qr, fl, lb):
    q = qr * jax.nn.sigmoid(qr) * HGRN_DK ** -0.5
    log_f = jnp.log(lb + (1.0 - lb) * jax.nn.sigmoid(fl))
    k = (1.0 - lb) * jax.nn.sigmoid(-fl)
    return q, k, log_f


def _hgrn_out(o, gr, nw):
    o = o * lax.rsqrt(jnp.mean(o * o, axis=-1, keepdims=True) + RMS_EPS) * nw
    return o * (gr * jax.nn.sigmoid(gr))


def _hgrn_prompt_kernel(q_ref, f_ref, i_ref, g_ref, lb_ref, nw_ref, o_ref, s_ref, st_sc, *, c, m, layer):
    ci = pl.program_id(2)

    @pl.when(ci == 0)
    def _():
        st_sc[...] = jnp.zeros(st_sc.shape, F32)

    lb = _hgrn_lower_bound(lb_ref, layer)
    q, k, log_f = _hgrn_gates(q_ref[...], f_ref[...], lb)
    v = i_ref[...]
    vb = v.astype(BF16)
    t_idx = lax.broadcasted_iota(jnp.int32, (c, c), 0)
    s_idx = lax.broadcasted_iota(jnp.int32, (c, c), 1)
    tril = (s_idx <= t_idx).astype(BF16)
    hi, lo = _split_hi_lo(log_f)
    G = _dot(tril, hi) + _dot(tril, lo)

    st = st_sc[...]
    o = _dot_nt((q * jnp.exp(G)).astype(BF16), st.astype(BF16))

    pieces = [jnp.zeros((m, HGRN_DV), F32)]
    for blk in range(1, c // m):
        lo_r, hi_r = blk * m, (blk + 1) * m
        g_ref_row = G[lo_r - 1:lo_r, :]
        qt = q[lo_r:hi_r] * jnp.exp(G[lo_r:hi_r] - g_ref_row)
        kt = k[0:lo_r] * jnp.exp(g_ref_row - G[0:lo_r])
        sc = _dot_nt(qt.astype(BF16), kt.astype(BF16))
        pieces.append(_dot(sc.astype(BF16), vb[0:lo_r]))
    o = o + jnp.concatenate(pieces, axis=0)

    in_blk = lax.broadcasted_iota(jnp.int32, (c, 1), 0) % m
    for d in range(m):
        ks = k if d == 0 else pltpu.roll(k, d, 0)
        gs = G if d == 0 else pltpu.roll(G, d, 0)
        vs = v if d == 0 else pltpu.roll(v, d, 0)
        a = jnp.sum(q * ks * jnp.exp(jnp.minimum(G - gs, 0.0)), axis=1, keepdims=True)
        o = o + jnp.where(in_blk >= d, a, 0.0) * vs

    o_ref[...] = _hgrn_out(o, g_ref[...], nw_ref[...]).astype(o_ref.dtype)

    g_end = G[c - 1:c, :]
    kd = k * jnp.exp(g_end - G)
    st_new = st * jnp.exp(g_end) + _dot_tn(vb, kd.astype(BF16))
    st_sc[...] = st_new

    @pl.when(ci == pl.num_programs(2) - 1)
    def _():
        s_ref[...] = st_new.T


def _hgrn_prompt(h4, lbs_raw, norm_w, batch, t, layer, c=128, m=16):
    c = min(c, t)
    nc = t // c
    col = lambda off: (lambda b, h, i: (b * nc + i, off + h))
    return pl.pallas_call(
        functools.partial(_hgrn_prompt_kernel, c=c, m=m, layer=layer),
        grid=(batch, H_C, nc),
        in_specs=[pl.BlockSpec((c, HGRN_DK), col(0)), pl.BlockSpec((c, HGRN_DK), col(H_C)),
                  pl.BlockSpec((c, HGRN_DV), col(2 * H_C)), pl.BlockSpec((c, HGRN_DV), col(3 * H_C)),
                  pl.BlockSpec((lbs_raw.shape[0], HGRN_DK), lambda b, h, i: (0, h)),
                  pl.BlockSpec((1, HGRN_DV), lambda b, h, i: (0, 0))],
        out_specs=[pl.BlockSpec((c, HGRN_DV), lambda b, h, i: (b * nc + i, h)),
                   pl.BlockSpec((None, None, HGRN_DK, HGRN_DV), lambda b, h, i: (b, h, 0, 0))],
        out_shape=[jax.ShapeDtypeStruct((batch * t, D_MODEL), BF16),
                   jax.ShapeDtypeStruct((batch, H_C, HGRN_DK, HGRN_DV), F32)],
        scratch_shapes=[pltpu.VMEM((HGRN_DV, HGRN_DK), F32)],
        compiler_params=_cparams("parallel", "parallel", "arbitrary"),
        name="hgrn_prompt",
    )(h4, h4, h4, h4, lbs_raw, norm_w.reshape(1, HGRN_DV))


def _lane_to_sublane(row):
    n = row.shape[1]
    eye = lax.broadcasted_iota(jnp.int32, (n, n), 0) == lax.broadcasted_iota(jnp.int32, (n, n), 1)
    return jnp.sum(jnp.where(eye, jnp.broadcast_to(row, (n, n)), 0.0), axis=1, keepdims=True)


def _hgrn_decode_kernel(h_ref, s0_ref, lb_ref, nw_ref, o_ref, s_ref, *, layer):
    lb_all = _hgrn_lower_bound(lb_ref, layer)
    nw = nw_ref[...]
    for h in range(H_C):
        sl = lambda part: slice((part * H_C + h) * HGRN_DK, (part * H_C + h + 1) * HGRN_DK)
        q, k, log_f = _hgrn_gates(h_ref[:, sl(0)], h_ref[:, sl(1)], lb_all[:, h * HGRN_DK:(h + 1) * HGRN_DK])
        v = h_ref[:, sl(2)]
        s_new = _lane_to_sublane(jnp.exp(log_f)) * s0_ref[h] + _lane_to_sublane(k) * v
        s_ref[h] = s_new
        o = jnp.sum(_lane_to_sublane(q) * s_new, axis=0, keepdims=True)
        o_ref[:, h * HGRN_DV:(h + 1) * HGRN_DV] = _hgrn_out(o, h_ref[:, sl(3)], nw).astype(o_ref.dtype)


def _hgrn_decode(h4, s0, lbs_raw, norm_w, layer):
    n = h4.shape[0]
    state_spec = pl.BlockSpec((None, H_C, HGRN_DK, HGRN_DV), lambda b: (b, 0, 0, 0))
    o, s = pl.pallas_call(
        functools.partial(_hgrn_decode_kernel, layer=layer),
        grid=(n,),
        in_specs=[pl.BlockSpec((None, 1, 4 * D_MODEL), lambda b: (b, 0, 0)), state_spec,
                  pl.BlockSpec(lbs_raw.shape, lambda b: (0, 0)), pl.BlockSpec((1, HGRN_DV), lambda b: (0, 0))],
        out_specs=[pl.BlockSpec((None, 1, D_MODEL), lambda b: (b, 0, 0)), state_spec],
        out_shape=[jax.ShapeDtypeStruct((n, 1, D_MODEL), BF16), jax.ShapeDtypeStruct(s0.shape, F32)],
        compiler_params=_cparams("parallel"),
        name="hgrn_decode",
    )(h4.reshape(n, 1, 4 * D_MODEL), s0, lbs_raw, norm_w.reshape(1, HGRN_DV))
    return o.reshape(n, D_MODEL), s


MOE_TILE = 512


def _router_kernel(x_ref, w_ref, o_ref):
    logits = _dot(x_ref[...].astype(BF16), w_ref[...])
    lane = lax.broadcasted_iota(jnp.int32, logits.shape, 1)
    lg = jnp.where(lane < N_EXPERTS, logits, -jnp.inf)
    m1 = jnp.max(lg, axis=1, keepdims=True)
    i1 = jnp.min(jnp.where(lg == m1, lane, LANES), axis=1, keepdims=True)
    lg2 = jnp.where(lane == i1, -jnp.inf, lg)
    m2 = jnp.max(lg2, axis=1, keepdims=True)
    i2 = jnp.min(jnp.where(lg2 == m2, lane, LANES), axis=1, keepdims=True)
    e2 = jnp.exp(m2 - m1)
    g1 = 1.0 / (1.0 + e2)
    out = jnp.where(lane == 0, g1, jnp.where(lane == 1, e2 * g1, jnp.where(
        lane == 2, i1.astype(F32), jnp.where(lane == 3, i2.astype(F32), 0.0))))
    o_ref[...] = out


def _router(x, w_router_pad, tm=TOKEN_TILE):
    n = x.shape[0]
    return pl.pallas_call(
        _router_kernel,
        grid=(pl.cdiv(n, tm),),
        in_specs=[pl.BlockSpec((tm, D_MODEL), lambda i: (i, 0)), pl.BlockSpec((D_MODEL, LANES), lambda i: (0, 0))],
        out_specs=pl.BlockSpec((tm, LANES), lambda i: (i, 0)),
        out_shape=jax.ShapeDtypeStruct((n, LANES), F32),
        compiler_params=_cparams("parallel"),
        name="router",
    )(x, w_router_pad)


def _moe_in_kernel(te_ref, nu_ref, xs_ref, wa_ref, wb_ref, o_ref):
    used = pl.program_id(0) < nu_ref[0]

    @pl.when(used)
    def _():
        xb = xs_ref[...]
        a = _dot(xb, wa_ref[...])
        b = _dot(xb, wb_ref[...])
        o_ref[...] = (a * jax.nn.sigmoid(a) * b).astype(o_ref.dtype)

    @pl.when(jnp.logical_not(used))
    def _():
        o_ref[...] = jnp.zeros(o_ref.shape, o_ref.dtype)


def _moe_out_kernel(te_ref, nu_ref, h_ref, w_ref, o_ref):
    used = pl.program_id(0) < nu_ref[0]

    @pl.when(used)
    def _():
        o_ref[...] = _dot(h_ref[...], w_ref[...])

    @pl.when(jnp.logical_not(used))
    def _():
        o_ref[...] = jnp.zeros(o_ref.shape, o_ref.dtype)


def _moe_experts(xs, tile_expert, n_used, w_in, w_out, tn=512):
    p = xs.shape[0]
    f = w_out.shape[1]
    nj = f // tn
    nt = p // MOE_TILE
    h = pl.pallas_call(
        _moe_in_kernel,
        grid_spec=pltpu.PrefetchScalarGridSpec(
            num_scalar_prefetch=2, grid=(nt, nj),
            in_specs=[pl.BlockSpec((MOE_TILE, D_MODEL), lambda i, j, te, nu: (i, 0)),
                      pl.BlockSpec((None, D_MODEL, tn), lambda i, j, te, nu: (te[i], 0, j)),
                      pl.BlockSpec((None, D_MODEL, tn), lambda i, j, te, nu: (te[i], 0, j + nj))],
            out_specs=pl.BlockSpec((MOE_TILE, tn), lambda i, j, te, nu: (i, j))),
        out_shape=jax.ShapeDtypeStruct((p, f), BF16),
        compiler_params=_cparams("parallel", "arbitrary"),
        name="moe_in",
    )(tile_expert, n_used, xs, w_in, w_in)
    return pl.pallas_call(
        _moe_out_kernel,
        grid_spec=pltpu.PrefetchScalarGridSpec(
            num_scalar_prefetch=2, grid=(nt,),
            in_specs=[pl.BlockSpec((MOE_TILE, f), lambda i, te, nu: (i, 0)),
                      pl.BlockSpec((None, f, D_MODEL), lambda i, te, nu: (te[i], 0, 0))],
            out_specs=pl.BlockSpec((MOE_TILE, D_MODEL), lambda i, te, nu: (i, 0))),
        out_shape=jax.ShapeDtypeStruct((p, D_MODEL), F32),
        compiler_params=_cparams("parallel"),
        name="moe_out",
    )(tile_expert, n_used, h, w_out)


def _moe_plan(route):
    n = route.shape[0]
    e_idx = route[:, 2:4].astype(jnp.int32).reshape(-1)
    onehot = (e_idx[:, None] == jnp.arange(N_EXPERTS, dtype=jnp.int32)[None, :]).astype(jnp.int32)
    running = jnp.cumsum(onehot, axis=0)
    rank = jnp.take_along_axis(running, e_idx[:, None], axis=1)[:, 0] - 1
    counts = running[-1]
    tiles = (counts + MOE_TILE - 1) // MOE_TILE
    tile_end = jnp.cumsum(tiles)
    tile_start = tile_end - tiles
    nt = (2 * n + N_EXPERTS * (MOE_TILE - 1) + MOE_TILE - 1) // MOE_TILE
    dest = tile_start[e_idx] * MOE_TILE + rank
    token = jnp.zeros((nt * MOE_TILE,), jnp.int32).at[dest].set(jnp.arange(2 * n, dtype=jnp.int32) // 2)
    tile_expert = jnp.minimum(
        jnp.searchsorted(tile_end, jnp.arange(nt, dtype=jnp.int32), side="right"), N_EXPERTS - 1).astype(jnp.int32)
    return token, dest.reshape(n, 2), tile_expert, tile_end[-1:].astype(jnp.int32)


def _moe(x1, w_router_pad, w_in, w_out):
    route = _router(x1, w_router_pad)
    token, pos, tile_expert, n_used = _moe_plan(route)
    xs = x1.astype(BF16)[token]
    ys = _moe_experts(xs, tile_expert, n_used, w_in, w_out)
    return ys[pos[:, 0]], ys[pos[:, 1]], route


def kernel(x_prompt, x_sample, cache_attn, state_hgrn, page_table, p_prompt, p_sample, w_in_even, w_out_even,
           w_in_odd, w_out_odd, hgrn_lower_bounds, hgrn_norm_w, ln_mix_g, ln_mix_b, ln_ffn_g, ln_ffn_b,
           w_ffn_in, w_ffn_out, w_router, w_moe_in, w_moe_out, w_ple_proj, w_ple_gate):
    batch, t, d = x_prompt.shape
    n_seq = x_sample.shape[0]
    n = batch * t
    past_len = page_table.shape[1] * PAGE_SIZE
    x = jnp.concatenate([x_prompt.reshape(n, d), x_sample.reshape(n_seq, d)], axis=0)
    pos = jnp.concatenate([jnp.tile(jnp.arange(t, dtype=jnp.int32), batch), jnp.full((n_seq,), past_len, jnp.int32)])
    cs = _rotary_table(pos)
    p_all = jnp.concatenate([p_prompt.reshape(DEPTH, n, PLE_DIM), p_sample.reshape(DEPTH, n_seq, PLE_DIM)], axis=1)
    join = lambda a, b: jnp.concatenate([a, b], axis=0)

    rows_p, rows_s, state_p, state_s = [], [], [], []
    for i in range(DEPTH):
        j = i // 2
        ple = (p_all[i], w_ple_gate[i].astype(BF16), w_ple_proj[i].astype(BF16))
        if i % 2 == 0:
            w_rows, w_q = _prep_even_in(w_in_even[j])
            rows, qa, qi, wi, qb, ka, va, ki, kb, vb = _even_inproj(x, w_rows, w_q, cs)
            oa_p = _dsa_prompt(qa, qi, wi, ka, va, ki, batch, t)
            ob_p = _sb_prompt(qb, kb, vb, batch, t)
            oa_s, ob_s = _even_decode(page_table, cache_attn[j], qa[n:], qi[n:], wi[n:], qb[n:], rows[n:])
            w_o = w_out_even[j].astype(BF16)
            x1 = _mix_ln(x, ln_mix_g[i], ln_mix_b[i],
                         mm=[(join(oa_p, oa_s), w_o[:WIDTH_A]), (join(ob_p, ob_s), w_o[WIDTH_A:])])
            h = _swiglu_in(x1, w_ffn_in[j].astype(BF16))
            x = _mix_ln(x1, ln_ffn_g[i], ln_ffn_b[i], mm=[(h, w_ffn_out[j].astype(BF16))], ple=ple)
            rows_p.append(rows[:n].reshape(batch, t, ROW_DIM))
            rows_s.append(rows[n:].reshape(n_seq, 1, ROW_DIM))
        else:
            h4 = _matmul(x, w_in_odd[j].astype(BF16))
            o_p, s_p = _hgrn_prompt(h4, hgrn_lower_bounds, hgrn_norm_w[j], batch, t, j)
            o_s, s_s = _hgrn_decode(h4[n:], state_hgrn[j], hgrn_lower_bounds, hgrn_norm_w[j], j)
            x1 = _mix_ln(x, ln_mix_g[i], ln_mix_b[i], mm=[(join(o_p, o_s), w_out_odd[j].astype(BF16))])
            w_r = jnp.pad(w_router[j], ((0, 0), (0, LANES - N_EXPERTS))).astype(BF16)
            y1, y2, route = _moe(x1, w_r, w_moe_in[j].astype(BF16), w_moe_out[j].astype(BF16))
            x = _mix_ln(x1, ln_ffn_g[i], ln_ffn_b[i], comb=(y1, y2), gates=route, ple=ple)
            state_p.append(s_p)
            state_s.append(s_s)
    return (x[:n].reshape(batch, t, d), x[n:].reshape(n_seq, 1, d), jnp.stack(rows_p), jnp.stack(rows_s),
            jnp.stack(state_p), jnp.stack(state_s))
```

```python
import functools

import jax
import jax.numpy as jnp
import numpy as np
from jax import lax
from jax.experimental import pallas as pl
from jax.experimental.pallas import tpu as pltpu

D_MODEL = 1024
DEPTH = 4
PAGE_SIZE = 128
HEAD_DIM = 64
N_HEADS_A = 8
N_HEADS_B = 8
IDX_HEADS = 4
IDX_DIM = 64
TOPK_MAX = 256
ROPE_THETA = 500000.0
H_C = 8
HGRN_DK = 128
HGRN_DV = D_MODEL // H_C
D_FF = 2816
N_EXPERTS = 8
D_FF_EXPERT = 3584
PLE_DIM = 256
DEEPNORM_ALPHA = (2 * DEPTH) ** 0.25
LN_EPS = 1e-5
RMS_EPS = 1e-6
NEG_BIG = -1e30
WIDTH_A = N_HEADS_A * HEAD_DIM
WIDTH_B = N_HEADS_B * HEAD_DIM
ROW_DIM = 2 * WIDTH_A + IDX_DIM + 2 * WIDTH_B

LANES = 128
HALF = LANES // 2
VMEM_LIMIT = 56 * 1024 * 1024
TOKEN_TILE = 384
INT_MIN = -2 ** 31
NO_INDEX = 2 ** 30

BF16 = jnp.bfloat16
F32 = jnp.float32


def _cparams(*sem, vmem=VMEM_LIMIT):
    return pltpu.CompilerParams(dimension_semantics=sem, vmem_limit_bytes=vmem)


def _dot(a, b):
    return jnp.dot(a, b, preferred_element_type=F32)


def _dot_nt(a, b):
    return lax.dot_general(a, b, (((1,), (1,)), ((), ())), preferred_element_type=F32)


def _dot_tn(a, b):
    return lax.dot_general(a, b, (((0,), (0,)), ((), ())), preferred_element_type=F32)


def _low_half(shape):
    return lax.broadcasted_iota(jnp.int32, shape, len(shape) - 1) < HALF


def _split_hi_lo(x):
    hi = x.astype(BF16)
    lo = (x - hi.astype(F32)).astype(BF16)
    return hi, lo


def _rotary(t, c, s1, s2):
    return t * c + pltpu.roll(t, LANES - 8, 1) * s1 + pltpu.roll(t, 8, 1) * s2


def _even_inproj_kernel(x_ref, wr_ref, wq_ref, cs_ref, rows_ref, qa_ref, qi_ref, wi_ref, qb_ref,
                        ka_ref, va_ref, ki_ref, kb_ref, vb_ref):
    xb = x_ref[...].astype(BF16)
    c = cs_ref[:, 0:LANES]
    s1 = cs_ref[:, LANES:2 * LANES]
    s2 = cs_ref[:, 2 * LANES:3 * LANES]
    low = _low_half(c.shape)
    acc = _dot(xb, wr_ref[...])
    for t in range(WIDTH_A // LANES):
        sl = slice(t * LANES, (t + 1) * LANES)
        r = _rotary(acc[:, sl], c, s1, s2)
        rows_ref[:, sl] = r
        ka_ref[:, sl] = r.astype(BF16)
    v_a = acc[:, WIDTH_A:2 * WIDTH_A]
    rows_ref[:, WIDTH_A:2 * WIDTH_A] = v_a
    va_ref[...] = v_a.astype(BF16)
    o_i = 2 * WIDTH_A
    t8 = acc[:, o_i:o_i + LANES]
    r8 = _rotary(t8, jnp.where(low, c, 1.0), jnp.where(low, s1, 0.0), jnp.where(low, s2, 0.0))
    rows_ref[:, o_i:o_i + LANES] = r8
    ki_ref[...] = jnp.where(low, r8, pltpu.roll(r8, HALF, 1)).astype(BF16)
    rows_ref[:, o_i + LANES:ROW_DIM] = acc[:, o_i + LANES:ROW_DIM]
    o_b = o_i + IDX_DIM
    kb_ref[...] = acc[:, o_b:o_b + WIDTH_B].astype(BF16)
    vb_ref[...] = acc[:, o_b + WIDTH_B:o_b + 2 * WIDTH_B].astype(BF16)

    accq = _dot(xb, wq_ref[...])
    scale = HEAD_DIM ** -0.5

    def put_pairs(dst_ref, src_off, n_pairs, rot):
        for p in range(n_pairs):
            t = accq[:, src_off + p * LANES:src_off + (p + 1) * LANES]
            if rot:
                t = _rotary(t, c, s1, s2)
            t = t * scale
            dst_ref[:, (2 * p) * LANES:(2 * p + 1) * LANES] = jnp.where(low, t, 0.0).astype(BF16)
            dst_ref[:, (2 * p + 1) * LANES:(2 * p + 2) * LANES] = jnp.where(low, 0.0, t).astype(BF16)

    put_pairs(qa_ref, 0, N_HEADS_A // 2, True)
    put_pairs(qi_ref, WIDTH_A, IDX_HEADS // 2, True)
    put_pairs(qb_ref, WIDTH_A + IDX_HEADS * IDX_DIM, N_HEADS_B // 2, False)
    o_w = WIDTH_A + IDX_HEADS * IDX_DIM + WIDTH_B
    wi_ref[...] = accq[:, o_w:o_w + LANES] * IDX_HEADS ** -0.5


def _even_inproj(x, w_rows, w_q, cs, tm=TOKEN_TILE):
    n = x.shape[0]
    qw = w_q.shape[1]
    row = lambda i: (i, 0)
    const = lambda i: (0, 0)
    outs = [
        (ROW_DIM, F32), (2 * WIDTH_A, BF16), (2 * IDX_HEADS * IDX_DIM, BF16), (LANES, F32), (2 * WIDTH_B, BF16),
        (WIDTH_A, BF16), (WIDTH_A, BF16), (LANES, BF16), (WIDTH_B, BF16), (WIDTH_B, BF16),
    ]
    return pl.pallas_call(
        _even_inproj_kernel,
        grid=(pl.cdiv(n, tm),),
        in_specs=[pl.BlockSpec((tm, D_MODEL), row), pl.BlockSpec((D_MODEL, ROW_DIM), const),
                  pl.BlockSpec((D_MODEL, qw), const), pl.BlockSpec((tm, 3 * LANES), row)],
        out_specs=[pl.BlockSpec((tm, w), row) for w, _ in outs],
        out_shape=[jax.ShapeDtypeStruct((n, w), d) for w, d in outs],
        compiler_params=_cparams("parallel"),
        name="even_inproj",
    )(x, w_rows, w_q, cs)


def _matmul_kernel(x_ref, w_ref, o_ref):
    o_ref[...] = _dot(x_ref[...].astype(BF16), w_ref[...]).astype(o_ref.dtype)


def _matmul(x, w, out_dtype=F32, tm=TOKEN_TILE, tn=1024):
    n, k = x.shape
    m = w.shape[1]
    tn = min(tn, m)
    return pl.pallas_call(
        _matmul_kernel,
        grid=(pl.cdiv(n, tm), m // tn),
        in_specs=[pl.BlockSpec((tm, k), lambda i, j: (i, 0)), pl.BlockSpec((k, tn), lambda i, j: (0, j))],
        out_specs=pl.BlockSpec((tm, tn), lambda i, j: (i, j)),
        out_shape=jax.ShapeDtypeStruct((n, m), out_dtype),
        compiler_params=_cparams("parallel", "arbitrary"),
        name="matmul",
    )(x, w)


def _swiglu_in_kernel(x_ref, wa_ref, wb_ref, o_ref):
    xb = x_ref[...].astype(BF16)
    a = _dot(xb, wa_ref[...])
    b = _dot(xb, wb_ref[...])
    o_ref[...] = (a * jax.nn.sigmoid(a) * b).astype(o_ref.dtype)


def _swiglu_in(x, w_in, tm=TOKEN_TILE, tn=1408):
    n, k = x.shape
    f = w_in.shape[1] // 2
    nj = f // tn
    return pl.pallas_call(
        _swiglu_in_kernel,
        grid=(pl.cdiv(n, tm), nj),
        in_specs=[pl.BlockSpec((tm, k), lambda i, j: (i, 0)),
                  pl.BlockSpec((k, tn), lambda i, j: (0, j)),
                  pl.BlockSpec((k, tn), lambda i, j: (0, j + nj))],
        out_specs=pl.BlockSpec((tm, tn), lambda i, j: (i, j)),
        out_shape=jax.ShapeDtypeStruct((n, f), BF16),
        compiler_params=_cparams("parallel", "arbitrary"),
        name="swiglu_in",
    )(x, w_in, w_in)


def _layernorm(y, g, b):
    mu = jnp.mean(y, axis=-1, keepdims=True)
    d = y - mu
    var = jnp.mean(d * d, axis=-1, keepdims=True)
    return d * lax.rsqrt(var + LN_EPS) * g + b


def _mix_ln_kernel(n_mm, n_comb, ple, *refs):
    it = iter(refs)
    mm = [(next(it), next(it)) for _ in range(n_mm)]
    comb = [next(it) for _ in range(n_comb)]
    gates_ref = next(it) if n_comb else None
    x_ref, g_ref, b_ref = next(it), next(it), next(it)
    if ple:
        p_ref, wg_ref, wp_ref = next(it), next(it), next(it)
    o_ref = next(it)
    y = DEEPNORM_ALPHA * x_ref[...]
    for h_ref, w_ref in mm:
        y = y + _dot(h_ref[...], w_ref[...])
    for k, y_ref in enumerate(comb):
        y = y + gates_ref[:, k:k + 1] * y_ref[...]
    y = _layernorm(y, g_ref[...], b_ref[...])
    if ple:
        gate = jax.nn.sigmoid(_dot(y.astype(BF16), wg_ref[...]))
        y = y + gate * _dot(p_ref[...].astype(BF16), wp_ref[...])
    o_ref[...] = y


def _mix_ln(x, g, b, mm=(), comb=(), gates=None, ple=None, tm=TOKEN_TILE):
    n = x.shape[0]
    row = lambda i: (i, 0)
    const = lambda i: (0, 0)
    args, specs = [], []
    for h, w in mm:
        args += [h, w]
        specs += [pl.BlockSpec((tm, h.shape[1]), row), pl.BlockSpec(w.shape, const)]
    for y in comb:
        args.append(y)
        specs.append(pl.BlockSpec((tm, D_MODEL), row))
    if comb:
        args.append(gates)
        specs.append(pl.BlockSpec((tm, gates.shape[1]), row))
    args += [x, g.reshape(1, D_MODEL), b.reshape(1, D_MODEL)]
    specs += [pl.BlockSpec((tm, D_MODEL), row), pl.BlockSpec((1, D_MODEL), const), pl.BlockSpec((1, D_MODEL), const)]
    if ple is not None:
        p, wg, wp = ple
        args += [p, wg, wp]
        specs += [pl.BlockSpec((tm, PLE_DIM), row), pl.BlockSpec(wg.shape, const), pl.BlockSpec(wp.shape, const)]
    return pl.pallas_call(
        functools.partial(_mix_ln_kernel, len(mm), len(comb), ple is not None),
        grid=(pl.cdiv(n, tm),),
        in_specs=specs,
        out_specs=pl.BlockSpec((tm, D_MODEL), row),
        out_shape=jax.ShapeDtypeStruct((n, D_MODEL), F32),
        compiler_params=_cparams("parallel"),
        name="mix_ln",
    )(*args)


def _prep_even_in(w):
    sizes = (WIDTH_A, WIDTH_A, WIDTH_A, IDX_HEADS * IDX_DIM, IDX_DIM, IDX_HEADS, WIDTH_B, WIDTH_B, WIDTH_B)
    qa, ka, va, qi, ki, wi, qb, kb, vb = jnp.split(w, np.cumsum(sizes)[:-1].tolist(), axis=1)
    w_rows = jnp.concatenate([ka, va, ki, kb, vb], axis=1)
    pad = jnp.zeros((w.shape[0], LANES - IDX_HEADS), w.dtype)
    w_q = jnp.concatenate([qa, qi, qb, wi, pad], axis=1)
    return w_rows.astype(BF16), w_q.astype(BF16)


def _rotary_table(pos):
    rd = HEAD_DIM // 4
    half = rd // 2
    inv_freq = ROPE_THETA ** (-jnp.arange(half, dtype=F32) * 2.0 / rd)
    ang = pos.astype(F32)[:, None] * inv_freq[None, :]
    cos, sin = jnp.cos(ang), jnp.sin(ang)
    n = pos.shape[0]
    ones = jnp.ones((n, HEAD_DIM - rd), F32)
    zeros = jnp.zeros((n, HEAD_DIM - rd), F32)
    zh = jnp.zeros((n, half), F32)
    c = jnp.concatenate([cos, cos, ones], axis=1)
    s1 = jnp.concatenate([-sin, zh, zeros], axis=1)
    s2 = jnp.concatenate([zh, sin, zeros], axis=1)
    return jnp.concatenate([c, c, s1, s1, s2, s2], axis=1)


def _softplus(z):
    return jnp.maximum(z, 0.0) + jnp.log(1.0 + jnp.exp(-jnp.abs(z)))


def _sb_tiles(qs, kvs, upper2, state, valid):
    work = [(c, kt, vt) for kt, vt in kvs for c in range(len(qs))]
    zs = [_dot_nt(qs[c], kt) for c, kt, _ in work]
    sps = [_softplus(z) for z in zs]
    sp_ms = sps if valid is None else [jnp.where(valid, sp, 0.0) for sp in sps]
    sums = [_dot(jnp.concatenate(_split_hi_lo(sp_m), axis=1), upper2) for sp_m in sp_ms]
    carries = [carry for carry, _ in state]
    ws = []
    for (c, _, _), z, sp, sp_m, x in zip(work, zs, sps, sp_ms, sums):
        w = jnp.exp(z - sp + (carries[c] - x))
        ws.append((w if valid is None else jnp.where(valid, w, 0.0)).astype(BF16))
        carries[c] = carries[c] - jnp.sum(sp_m, axis=1, keepdims=True)
    accs = [acc for _, acc in state]
    for (c, _, vt), w in zip(work, ws):
        accs[c] = accs[c] + _dot(w, vt)
    return tuple(zip(carries, accs))


def _sb_prompt_kernel(q_ref, k_ref, v_ref, o_ref, *, tq, unroll):
    qi = pl.program_id(2)
    rows = lax.broadcasted_iota(jnp.int32, (tq, tq), 0)
    cols = lax.broadcasted_iota(jnp.int32, (tq, tq), 1)
    upper = (rows > cols).astype(BF16)
    upper2 = jnp.concatenate([upper, upper], axis=0)
    qs = [q_ref[:, 0:LANES], q_ref[:, LANES:2 * LANES]]

    def tiles(j_first, n, state, valid):
        kvs = []
        for u in range(n):
            start = pl.multiple_of((j_first - u) * tq, tq)
            kvs.append((k_ref[pl.ds(start, tq), :], v_ref[pl.ds(start, tq), :]))
        return _sb_tiles(qs, kvs, upper2, state, valid)

    zero = (jnp.zeros((tq, 1), F32), jnp.zeros((tq, LANES), F32))
    state = tiles(qi, 1, (zero, zero), cols < rows)
    n_left = qi % unroll
    for u in range(1, unroll):
        state = lax.cond(n_left >= u, lambda st, u=u: tiles(qi - u, 1, st, None), lambda st: st, state)
    first = qi - 1 - n_left
    state = lax.fori_loop(0, qi // unroll, lambda jj, st: tiles(first - unroll * jj, unroll, st, None), state)
    low = _low_half((tq, LANES))
    o_ref[...] = jnp.where(low, state[0][1], state[1][1]).astype(o_ref.dtype)


def _sb_prompt(q_pad, k, v, batch, t, tq=256, unroll=2):
    tq = min(tq, t)
    nq = t // tq
    return pl.pallas_call(
        functools.partial(_sb_prompt_kernel, tq=tq, unroll=unroll),
        grid=(batch, N_HEADS_B // 2, nq),
        in_specs=[pl.BlockSpec((tq, 2 * LANES), lambda b, p, i: (b * nq + i, p)),
                  pl.BlockSpec((t, LANES), lambda b, p, i: (b, p)),
                  pl.BlockSpec((t, LANES), lambda b, p, i: (b, p))],
        out_specs=pl.BlockSpec((tq, LANES), lambda b, p, i: (b * nq + i, p)),
        out_shape=jax.ShapeDtypeStruct((batch * t, WIDTH_B), BF16),
        compiler_params=_cparams("parallel", "parallel", "arbitrary"),
        name="sb_prompt",
    )(q_pad, k, v)


def _sortable(x):
    b = lax.bitcast_convert_type(x + 0.0, jnp.int32)
    return b ^ ((b >> 31) & 0x7FFFFFFF)


def _dsa_prompt_kernel(qa_ref, qi_ref, wi_ref, ka_ref, va_ref, ki_ref, o_ref, key_sc, bias_sc, mx_sc, acc_sc,
                       *, tq, tk, n_sel, t):
    q0 = pl.program_id(1) * tq
    nkt = (q0 + tq + tk - 1) // tk
    rows = q0 + lax.broadcasted_iota(jnp.int32, (tq, tk), 0)
    cols0 = lax.broadcasted_iota(jnp.int32, (tq, tk), 1)
    w = wi_ref[...]

    def score_tile(j, carry):
        start = pl.multiple_of(j * tk, tk)
        kt = ki_ref[pl.ds(start, tk), :]
        ss = [_dot_nt(qi_ref[:, h * LANES:(h + 1) * LANES], kt) for h in range(IDX_HEADS)]
        score = jnp.zeros((tq, tk), F32)
        for h in range(IDX_HEADS):
            score = score + jnp.maximum(ss[h], 0.0) * w[:, h:h + 1]
        score = jnp.where(start + cols0 <= rows, score, NEG_BIG)
        key_sc[:, pl.ds(start, tk)] = _sortable(score)
        return carry

    lax.fori_loop(0, nkt, score_tile, 0)

    def count(pred):
        def body(j, c):
            start = pl.multiple_of(j * tk, tk)
            return c + jnp.where(pred(key_sc[:, pl.ds(start, tk)], start + cols0), 1, 0)
        c = lax.fori_loop(0, nkt, body, jnp.zeros((tq, tk), jnp.int32))
        return jnp.sum(c, axis=1, keepdims=True)

    def bit_step(i, v):
        cand = v ^ lax.shift_left(jnp.int32(1), 31 - i)
        return jnp.where(count(lambda kk, kp: kk >= cand) >= n_sel, cand, v)

    v = lax.fori_loop(0, 32, bit_step, jnp.full((tq, 1), INT_MIN, jnp.int32))
    r = n_sel - count(lambda kk, kp: kk > v)
    n_eq = count(lambda kk, kp: kk == v)

    def tie_cut():
        nbits = (t - 1).bit_length()

        def step(i, c):
            cand = c + lax.shift_left(jnp.int32(1), nbits - 1 - i)
            g = count(lambda kk, kp: jnp.where(kk == v, kp, NO_INDEX) < cand)
            return jnp.where(g < r, cand, c)

        return lax.fori_loop(0, nbits, step, jnp.zeros((tq, 1), jnp.int32))

    cut = lax.cond(jnp.max(n_eq - r) > 0, tie_cut, lambda: jnp.full((tq, 1), t, jnp.int32))

    heads = range(N_HEADS_A)

    def logits(start):
        kts = [ka_ref[pl.ds(start, tk), p * LANES:(p + 1) * LANES] for p in range(N_HEADS_A // 2)]
        return [_dot_nt(qa_ref[:, h * LANES:(h + 1) * LANES], kts[h // 2]) for h in heads]

    mx_sc[...] = jnp.full(mx_sc.shape, NEG_BIG, F32)

    def max_tile(j, carry):
        start = pl.multiple_of(j * tk, tk)
        kk = key_sc[:, pl.ds(start, tk)]
        kp = start + cols0
        tie_ok = jnp.where(kk == v, kp, NO_INDEX) <= cut
        bias = jnp.where(kp <= rows, jnp.where(kk > v, 0.0, jnp.where(tie_ok, 0.0, NEG_BIG)), NEG_BIG)
        bias_sc[:, pl.ds(start, tk)] = bias
        for h, s in zip(heads, logits(start)):
            mx_sc[h] = jnp.maximum(mx_sc[h], s + bias)
        return carry

    lax.fori_loop(0, nkt, max_tile, 0)
    m = [jnp.max(mx_sc[h], axis=1, keepdims=True) for h in heads]
    mx_sc[...] = jnp.zeros(mx_sc.shape, F32)
    acc_sc[...] = jnp.zeros(acc_sc.shape, F32)

    def sum_tile(j, carry):
        start = pl.multiple_of(j * tk, tk)
        bias = bias_sc[:, pl.ds(start, tk)]
        vts = [va_ref[pl.ds(start, tk), p * LANES:(p + 1) * LANES] for p in range(N_HEADS_A // 2)]
        prs = []
        for h, s in zip(heads, logits(start)):
            pr = jnp.exp(s + bias - m[h])
            mx_sc[h] = mx_sc[h] + pr
            prs.append(pr.astype(BF16))
        for h in heads:
            acc_sc[h] = acc_sc[h] + _dot(prs[h], vts[h // 2])
        return carry

    lax.fori_loop(0, nkt, sum_tile, 0)
    low = _low_half((tq, LANES))
    outs = [acc_sc[h] / jnp.sum(mx_sc[h], axis=1, keepdims=True) for h in heads]
    for p in range(N_HEADS_A // 2):
        o_ref[:, p * LANES:(p + 1) * LANES] = jnp.where(low, outs[2 * p], outs[2 * p + 1]).astype(o_ref.dtype)


def _dsa_prompt(qa_pad, qi_pad, wi, ka, va, ki2, batch, t, tq=128, tk=256):
    tq = min(tq, t)
    tk = min(tk, t)
    nq = t // tq
    n_sel = max(1, min(TOPK_MAX, t // 4))
    qrow = lambda b, i: (b * nq + i, 0)
    seq = lambda b, i: (b, 0)
    return pl.pallas_call(
        functools.partial(_dsa_prompt_kernel, tq=tq, tk=tk, n_sel=n_sel, t=t),
        grid=(batch, nq),
        in_specs=[pl.BlockSpec((tq, 2 * WIDTH_A), qrow), pl.BlockSpec((tq, 2 * IDX_HEADS * IDX_DIM), qrow),
                  pl.BlockSpec((tq, LANES), qrow), pl.BlockSpec((t, WIDTH_A), seq), pl.BlockSpec((t, WIDTH_A), seq),
                  pl.BlockSpec((t, LANES), seq)],
        out_specs=pl.BlockSpec((tq, WIDTH_A), qrow),
        out_shape=jax.ShapeDtypeStruct((batch * t, WIDTH_A), BF16),
        scratch_shapes=[pltpu.VMEM((tq, t), jnp.int32), pltpu.VMEM((tq, t), F32),
                        pltpu.VMEM((N_HEADS_A, tq, tk), F32), pltpu.VMEM((N_HEADS_A, tq, LANES), F32)],
        compiler_params=_cparams("parallel", "arbitrary"),
        name="dsa_prompt",
    )(qa_pad, qi_pad, wi, ka, va, ki2)


OFF_KI = 2 * WIDTH_A
OFF_KB = OFF_KI + IDX_DIM
OFF_VB = OFF_KB + WIDTH_B
VB_WIN = (OFF_VB // LANES) * LANES


def _head_rows(row, off, n_rows=8):
    w = row.shape[1]
    col = lax.broadcasted_iota(jnp.int32, (n_rows, w), 1)
    hrow = lax.broadcasted_iota(jnp.int32, (n_rows, w), 0)
    head_of_col = ((col - off + HEAD_DIM) >> 6) - 1
    return jnp.where(head_of_col == hrow, jnp.broadcast_to(row, (n_rows, w)), 0.0)


def _sum_all(x):
    return jnp.sum(jnp.sum(x, axis=0, keepdims=True), axis=1, keepdims=True)


def _even_decode_kernel(pt_ref, qa_ref, qi_ref, wi_ref, qb_ref, self_ref, *rest, n_pages, n_sel):
    pages = rest[:n_pages]
    oa_ref, ob_ref, key_sc, la_sc, z_sc = rest[n_pages:]
    n_past = n_pages * PAGE_SIZE
    f32_tile = lambda ref, h: ref[:, h * LANES:(h + 1) * LANES].astype(F32)
    swap = lambda t: pltpu.roll(t, HALF, 1)

    qa_row = jnp.concatenate([f32_tile(qa_ref, 2 * p) + f32_tile(qa_ref, 2 * p + 1)
                              for p in range(N_HEADS_A // 2)], axis=1)
    qa_bd = _head_rows(qa_row, 0)
    qa_bd16 = qa_bd.astype(BF16)
    zero_tile = jnp.zeros((1, LANES), F32)
    even_b = [swap(f32_tile(qb_ref, 2 * p)) for p in range(N_HEADS_B // 2)] + [zero_tile]
    odd_b = [zero_tile] + [swap(f32_tile(qb_ref, 2 * p + 1)) for p in range(N_HEADS_B // 2)]
    qb_row = jnp.concatenate([e + o for e, o in zip(even_b, odd_b)], axis=1)
    qb_bd16 = _head_rows(qb_row, OFF_KB - OFF_KI).astype(BF16)
    kb_win = qb_row.shape[1]
    hrow = lax.broadcasted_iota(jnp.int32, (8, LANES), 0)
    lane8 = lax.broadcasted_iota(jnp.int32, (8, LANES), 1)
    qi_rows = jnp.zeros((8, LANES), F32)
    for h in range(IDX_HEADS):
        t = f32_tile(qi_ref, h)
        t = swap(t) if h % 2 else t
        qi_rows = jnp.where(hrow == h, jnp.broadcast_to(t, (8, LANES)), qi_rows)
    qi_rows16 = qi_rows.astype(BF16)
    w_col = jnp.sum(jnp.where(lane8 == hrow, jnp.broadcast_to(wi_ref[...], (8, LANES)), 0.0), axis=1, keepdims=True)

    def idx_score(s_i):
        return jnp.sum(jnp.maximum(s_i, 0.0) * w_col, axis=0, keepdims=True)

    for s in range(n_pages):
        pg = pages[s]
        s_i = _dot_nt(qi_rows16, pg[:, OFF_KI:OFF_KI + LANES].astype(BF16))
        key_sc[s:s + 1, :] = _sortable(idx_score(s_i))
        la_sc[s] = _dot_nt(qa_bd16, pg[:, 0:WIDTH_A].astype(BF16))
        z_sc[s] = _dot_nt(qb_bd16, pg[:, OFF_KI:OFF_KI + kb_win].astype(BF16))

    rnd = lambda x: x.astype(BF16).astype(F32)
    s_self = jnp.sum(qi_rows16.astype(F32) * rnd(self_ref[:, OFF_KI:OFF_KI + LANES]), axis=1, keepdims=True)
    sc_self = idx_score(s_self)
    la_self = jnp.sum(qa_bd16.astype(F32) * rnd(self_ref[:, 0:WIDTH_A]), axis=1, keepdims=True)
    lane1 = lax.broadcasted_iota(jnp.int32, (1, LANES), 1)
    key_sc[n_pages:n_pages + 1, :] = jnp.where(lane1 == 0, _sortable(jnp.broadcast_to(sc_self, (1, LANES))), INT_MIN)

    keys = key_sc[...]
    kp = (lax.broadcasted_iota(jnp.int32, keys.shape, 0) * PAGE_SIZE
          + lax.broadcasted_iota(jnp.int32, keys.shape, 1))
    count = lambda pred: _sum_all(jnp.where(pred, 1, 0))

    def bit_step(i, v):
        cand = v ^ lax.shift_left(jnp.int32(1), 31 - i)
        return jnp.where(count(keys >= cand) >= n_sel, cand, v)

    v = lax.fori_loop(0, 32, bit_step, jnp.full((1, 1), INT_MIN, jnp.int32))
    r = n_sel - count(keys > v)
    n_eq = count(keys == v)
    tie_pos = jnp.where(keys == v, kp, NO_INDEX)

    def tie_cut():
        nbits = n_past.bit_length()

        def step(i, c):
            cand = c + lax.shift_left(jnp.int32(1), nbits - 1 - i)
            return jnp.where(count(tie_pos < cand) < r, cand, c)

        return lax.fori_loop(0, nbits, step, jnp.zeros((1, 1), jnp.int32))

    cut = lax.cond(jnp.max(n_eq - r) > 0, tie_cut, lambda: jnp.full((1, 1), NO_INDEX - 1, jnp.int32))
    bias = jnp.where(kp <= n_past, jnp.where(keys > v, 0.0, jnp.where(tie_pos <= cut, 0.0, NEG_BIG)), NEG_BIG)

    bias_self = bias[n_pages:n_pages + 1, 0:1]
    m = la_self + bias_self
    for s in range(n_pages):
        m = jnp.maximum(m, jnp.max(la_sc[s] + bias[s:s + 1, :], axis=1, keepdims=True))
    p_self = jnp.exp(la_self + bias_self - m)
    l = p_self
    acc = rnd(p_self) * rnd(self_ref[:, WIDTH_A:2 * WIDTH_A])
    for s in range(n_pages):
        pr = jnp.exp(la_sc[s] + bias[s:s + 1, :] - m)
        l = l + jnp.sum(pr, axis=1, keepdims=True)
        acc = acc + _dot(pr.astype(BF16), pages[s][:, WIDTH_A:2 * WIDTH_A].astype(BF16))
    oa_ref[...] = jnp.sum(_head_rows(jnp.ones((1, WIDTH_A), F32), 0) * (acc / l), axis=0,
                          keepdims=True).astype(oa_ref.dtype)

    t_idx = lax.broadcasted_iota(jnp.int32, (PAGE_SIZE, PAGE_SIZE), 0)
    s_idx = lax.broadcasted_iota(jnp.int32, (PAGE_SIZE, PAGE_SIZE), 1)
    upper = (t_idx > s_idx).astype(BF16)
    vb_win = ROW_DIM - VB_WIN
    carry = jnp.zeros((8, 1), F32)
    acc_b = jnp.zeros((8, vb_win), F32)
    for s in reversed(range(n_pages)):
        z = z_sc[s]
        sp = _softplus(z)
        hi, lo = _split_hi_lo(-sp)
        later = _dot(hi, upper) + _dot(lo, upper) + carry
        w = jnp.exp(z - sp + later)
        acc_b = acc_b + _dot(w.astype(BF16), pages[s][:, VB_WIN:ROW_DIM].astype(BF16))
        carry = carry - jnp.sum(sp, axis=1, keepdims=True)
    picked = jnp.sum(_head_rows(jnp.ones((1, vb_win), F32), OFF_VB - VB_WIN) * acc_b, axis=0, keepdims=True)
    ob_ref[...] = picked[:, OFF_VB - VB_WIN:OFF_VB - VB_WIN + WIDTH_B].astype(ob_ref.dtype)


def _even_decode(page_table, cache, layer, qa_pad, qi_pad, wi, qb_pad, self_rows):
    n, n_pages = page_table.shape
    n_sel = max(1, min(TOPK_MAX, (n_pages * PAGE_SIZE + 1) // 4))
    per_seq = lambda w: pl.BlockSpec((None, 1, w), lambda b, pt: (b, 0, 0))
    page_spec = lambda s: pl.BlockSpec((None, None, PAGE_SIZE, ROW_DIM), lambda b, pt: (layer, pt[b, s], 0, 0))
    r3 = lambda a: a.reshape(n, 1, a.shape[-1])
    oa, ob = pl.pallas_call(
        functools.partial(_even_decode_kernel, n_pages=n_pages, n_sel=n_sel),
        grid_spec=pltpu.PrefetchScalarGridSpec(
            num_scalar_prefetch=1, grid=(n,),
            in_specs=[per_seq(2 * WIDTH_A), per_seq(2 * IDX_HEADS * IDX_DIM), per_seq(LANES), per_seq(2 * WIDTH_B),
                      per_seq(ROW_DIM)] + [page_spec(s) for s in range(n_pages)],
            out_specs=[per_seq(WIDTH_A), per_seq(WIDTH_B)],
            scratch_shapes=[pltpu.VMEM((n_pages + 1, LANES), jnp.int32), pltpu.VMEM((n_pages, 8, LANES), F32),
                            pltpu.VMEM((n_pages, 8, LANES), F32)]),
        out_shape=[jax.ShapeDtypeStruct((n, 1, WIDTH_A), BF16), jax.ShapeDtypeStruct((n, 1, WIDTH_B), BF16)],
        compiler_params=_cparams("parallel"),
        name="even_decode",
    )(page_table, r3(qa_pad), r3(qi_pad), r3(wi), r3(qb_pad), r3(self_rows), *([cache] * n_pages))
    return oa.reshape(n, WIDTH_A), ob.reshape(n, WIDTH_B)


def _hgrn_lower_bound(lb_ref, layer):
    x = lb_ref[...]
    e = jnp.exp(x - jnp.max(x, axis=0, keepdims=True))
    sm = e / jnp.sum(e, axis=0, keepdims=True)
    lb = jnp.zeros((1, sm.shape[1]), F32)
    for r in range(1, layer + 1):
        lb = lb + sm[r:r + 1, :]
    return lb


def _hgrn_gates(qr, fl, lb):
    q = qr * jax.nn.sigmoid(qr) * HGRN_DK ** -0.5
    log_f = jnp.log(lb + (1.0 - lb) * jax.nn.sigmoid(fl))
    k = (1.0 - lb) * jax.nn.sigmoid(-fl)
    return q, k, log_f


def _hgrn_out(o, gr, nw):
    o = o * lax.rsqrt(jnp.mean(o * o, axis=-1, keepdims=True) + RMS_EPS) * nw
    return o * (gr * jax.nn.sigmoid(gr))


def _hgrn_prompt_kernel(q_ref, f_ref, i_ref, g_ref, lb_ref, nw_ref, o_ref, s_ref, st_sc, *, c, m, layer):
    ci = pl.program_id(2)

    @pl.when(ci == 0)
    def _():
        st_sc[...] = jnp.zeros(st_sc.shape, F32)

    lb = _hgrn_lower_bound(lb_ref, layer)
    q, k, log_f = _hgrn_gates(q_ref[...], f_ref[...], lb)
    v = i_ref[...]
    vb = v.astype(BF16)
    t_idx = lax.broadcasted_iota(jnp.int32, (c, c), 0)
    s_idx = lax.broadcasted_iota(jnp.int32, (c, c), 1)
    tril = (s_idx <= t_idx).astype(BF16)
    hi, lo = _split_hi_lo(log_f)
    G = _dot(tril, hi) + _dot(tril, lo)

    st = st_sc[...]
    o = _dot_nt((q * jnp.exp(G)).astype(BF16), st.astype(BF16))

    pieces = [jnp.zeros((m, HGRN_DV), F32)]
    for blk in range(1, c // m):
        lo_r, hi_r = blk * m, (blk + 1) * m
        g_ref_row = G[lo_r - 1:lo_r, :]
        qt = q[lo_r:hi_r] * jnp.exp(G[lo_r:hi_r] - g_ref_row)
        kt = k[0:lo_r] * jnp.exp(g_ref_row - G[0:lo_r])
        sc = _dot_nt(qt.astype(BF16), kt.astype(BF16))
        pieces.append(_dot(sc.astype(BF16), vb[0:lo_r]))
    o = o + jnp.concatenate(pieces, axis=0)

    in_blk = lax.broadcasted_iota(jnp.int32, (c, 1), 0) % m
    for d in range(m):
        ks = k if d == 0 else pltpu.roll(k, d, 0)
        gs = G if d == 0 else pltpu.roll(G, d, 0)
        vs = v if d == 0 else pltpu.roll(v, d, 0)
        a = jnp.sum(q * ks * jnp.exp(jnp.minimum(G - gs, 0.0)), axis=1, keepdims=True)
        o = o + jnp.where(in_blk >= d, a, 0.0) * vs

    o_ref[...] = _hgrn_out(o, g_ref[...], nw_ref[...]).astype(o_ref.dtype)

    g_end = G[c - 1:c, :]
    kd = k * jnp.exp(g_end - G)
    st_new = st * jnp.exp(g_end) + _dot_tn(vb, kd.astype(BF16))
    st_sc[...] = st_new

    @pl.when(ci == pl.num_programs(2) - 1)
    def _():
        s_ref[...] = st_new.T


def _hgrn_prompt(h4, lbs_raw, norm_w, batch, t, layer, c=128, m=16):
    c = min(c, t)
    nc = t // c
    col = lambda off: (lambda b, h, i: (b * nc + i, off + h))
    return pl.pallas_call(
        functools.partial(_hgrn_prompt_kernel, c=c, m=m, layer=layer),
        grid=(batch, H_C, nc),
        in_specs=[pl.BlockSpec((c, HGRN_DK), col(0)), pl.BlockSpec((c, HGRN_DK), col(H_C)),
                  pl.BlockSpec((c, HGRN_DV), col(2 * H_C)), pl.BlockSpec((c, HGRN_DV), col(3 * H_C)),
                  pl.BlockSpec((lbs_raw.shape[0], HGRN_DK), lambda b, h, i: (0, h)),
                  pl.BlockSpec((1, HGRN_DV), lambda b, h, i: (0, 0))],
        out_specs=[pl.BlockSpec((c, HGRN_DV), lambda b, h, i: (b * nc + i, h)),
                   pl.BlockSpec((None, None, HGRN_DK, HGRN_DV), lambda b, h, i: (b, h, 0, 0))],
        out_shape=[jax.ShapeDtypeStruct((batch * t, D_MODEL), BF16),
                   jax.ShapeDtypeStruct((batch, H_C, HGRN_DK, HGRN_DV), F32)],
        scratch_shapes=[pltpu.VMEM((HGRN_DV, HGRN_DK), F32)],
        compiler_params=_cparams("parallel", "parallel", "arbitrary"),
        name="hgrn_prompt",
    )(h4, h4, h4, h4, lbs_raw, norm_w.reshape(1, HGRN_DV))


def _lane_to_sublane(row):
    n = row.shape[1]
    eye = lax.broadcasted_iota(jnp.int32, (n, n), 0) == lax.broadcasted_iota(jnp.int32, (n, n), 1)
    return jnp.sum(jnp.where(eye, jnp.broadcast_to(row, (n, n)), 0.0), axis=1, keepdims=True)


def _hgrn_decode_kernel(h_ref, s0_ref, lb_ref, nw_ref, o_ref, s_ref, *, layer):
    lb_all = _hgrn_lower_bound(lb_ref, layer)
    nw = nw_ref[...]
    for h in range(H_C):
        sl = lambda part: slice((part * H_C + h) * HGRN_DK, (part * H_C + h + 1) * HGRN_DK)
        q, k, log_f = _hgrn_gates(h_ref[:, sl(0)], h_ref[:, sl(1)], lb_all[:, h * HGRN_DK:(h + 1) * HGRN_DK])
        v = h_ref[:, sl(2)]
        s_new = _lane_to_sublane(jnp.exp(log_f)) * s0_ref[h] + _lane_to_sublane(k) * v
        s_ref[h] = s_new
        o = jnp.sum(_lane_to_sublane(q) * s_new, axis=0, keepdims=True)
        o_ref[:, h * HGRN_DV:(h + 1) * HGRN_DV] = _hgrn_out(o, h_ref[:, sl(3)], nw).astype(o_ref.dtype)


def _hgrn_decode(h4, s0, lbs_raw, norm_w, layer):
    n = h4.shape[0]
    state_spec = pl.BlockSpec((None, H_C, HGRN_DK, HGRN_DV), lambda b: (b, 0, 0, 0))
    o, s = pl.pallas_call(
        functools.partial(_hgrn_decode_kernel, layer=layer),
        grid=(n,),
        in_specs=[pl.BlockSpec((None, 1, 4 * D_MODEL), lambda b: (b, 0, 0)), state_spec,
                  pl.BlockSpec(lbs_raw.shape, lambda b: (0, 0)), pl.BlockSpec((1, HGRN_DV), lambda b: (0, 0))],
        out_specs=[pl.BlockSpec((None, 1, D_MODEL), lambda b: (b, 0, 0)), state_spec],
        out_shape=[jax.ShapeDtypeStruct((n, 1, D_MODEL), BF16), jax.ShapeDtypeStruct(s0.shape, F32)],
        compiler_params=_cparams("parallel"),
        name="hgrn_decode",
    )(h4.reshape(n, 1, 4 * D_MODEL), s0, lbs_raw, norm_w.reshape(1, HGRN_DV))
    return o.reshape(n, D_MODEL), s


MOE_TILE = 512


def _router_kernel(x_ref, w_ref, o_ref):
    logits = _dot(x_ref[...].astype(BF16), w_ref[...])
    lane = lax.broadcasted_iota(jnp.int32, logits.shape, 1)
    lg = jnp.where(lane < N_EXPERTS, logits, -jnp.inf)
    m1 = jnp.max(lg, axis=1, keepdims=True)
    i1 = jnp.min(jnp.where(lg == m1, lane, LANES), axis=1, keepdims=True)
    lg2 = jnp.where(lane == i1, -jnp.inf, lg)
    m2 = jnp.max(lg2, axis=1, keepdims=True)
    i2 = jnp.min(jnp.where(lg2 == m2, lane, LANES), axis=1, keepdims=True)
    e2 = jnp.exp(m2 - m1)
    g1 = 1.0 / (1.0 + e2)
    out = jnp.where(lane == 0, g1, jnp.where(lane == 1, e2 * g1, jnp.where(
        lane == 2, i1.astype(F32), jnp.where(lane == 3, i2.astype(F32), 0.0))))
    o_ref[...] = out


def _router(x, w_router_pad, tm=TOKEN_TILE):
    n = x.shape[0]
    return pl.pallas_call(
        _router_kernel,
        grid=(pl.cdiv(n, tm),),
        in_specs=[pl.BlockSpec((tm, D_MODEL), lambda i: (i, 0)), pl.BlockSpec((D_MODEL, LANES), lambda i: (0, 0))],
        out_specs=pl.BlockSpec((tm, LANES), lambda i: (i, 0)),
        out_shape=jax.ShapeDtypeStruct((n, LANES), F32),
        compiler_params=_cparams("parallel"),
        name="router",
    )(x, w_router_pad)


def _moe_in_kernel(te_ref, nu_ref, xs_ref, wa_ref, wb_ref, o_ref):
    used = pl.program_id(0) < nu_ref[0]

    @pl.when(used)
    def _():
        xb = xs_ref[...]
        a = _dot(xb, wa_ref[...])
        b = _dot(xb, wb_ref[...])
        o_ref[...] = (a * jax.nn.sigmoid(a) * b).astype(o_ref.dtype)

    @pl.when(jnp.logical_not(used))
    def _():
        o_ref[...] = jnp.zeros(o_ref.shape, o_ref.dtype)


def _moe_out_kernel(te_ref, nu_ref, h_ref, w_ref, o_ref):
    used = pl.program_id(0) < nu_ref[0]

    @pl.when(used)
    def _():
        o_ref[...] = _dot(h_ref[...], w_ref[...])

    @pl.when(jnp.logical_not(used))
    def _():
        o_ref[...] = jnp.zeros(o_ref.shape, o_ref.dtype)


def _moe_experts(xs, tile_expert, n_used, w_in, w_out, tn=512):
    p = xs.shape[0]
    f = w_out.shape[1]
    nj = f // tn
    nt = p // MOE_TILE
    h = pl.pallas_call(
        _moe_in_kernel,
        grid_spec=pltpu.PrefetchScalarGridSpec(
            num_scalar_prefetch=2, grid=(nt, nj),
            in_specs=[pl.BlockSpec((MOE_TILE, D_MODEL), lambda i, j, te, nu: (i, 0)),
                      pl.BlockSpec((None, D_MODEL, tn), lambda i, j, te, nu: (te[i], 0, j)),
                      pl.BlockSpec((None, D_MODEL, tn), lambda i, j, te, nu: (te[i], 0, j + nj))],
            out_specs=pl.BlockSpec((MOE_TILE, tn), lambda i, j, te, nu: (i, j))),
        out_shape=jax.ShapeDtypeStruct((p, f), BF16),
        compiler_params=_cparams("parallel", "arbitrary"),
        name="moe_in",
    )(tile_expert, n_used, xs, w_in, w_in)
    return pl.pallas_call(
        _moe_out_kernel,
        grid_spec=pltpu.PrefetchScalarGridSpec(
            num_scalar_prefetch=2, grid=(nt,),
            in_specs=[pl.BlockSpec((MOE_TILE, f), lambda i, te, nu: (i, 0)),
                      pl.BlockSpec((None, f, D_MODEL), lambda i, te, nu: (te[i], 0, 0))],
            out_specs=pl.BlockSpec((MOE_TILE, D_MODEL), lambda i, te, nu: (i, 0))),
        out_shape=jax.ShapeDtypeStruct((p, D_MODEL), F32),
        compiler_params=_cparams("parallel"),
        name="moe_out",
    )(tile_expert, n_used, h, w_out)


def _moe_plan(route):
    n = route.shape[0]
    e_idx = route[:, 2:4].astype(jnp.int32).reshape(-1)
    onehot = (e_idx[:, None] == jnp.arange(N_EXPERTS, dtype=jnp.int32)[None, :]).astype(jnp.int32)
    running = jnp.cumsum(onehot, axis=0)
    rank = jnp.take_along_axis(running, e_idx[:, None], axis=1)[:, 0] - 1
    counts = running[-1]
    tiles = (counts + MOE_TILE - 1) // MOE_TILE
    tile_end = jnp.cumsum(tiles)
    tile_start = tile_end - tiles
    nt = (2 * n + N_EXPERTS * (MOE_TILE - 1) + MOE_TILE - 1) // MOE_TILE
    dest = tile_start[e_idx] * MOE_TILE + rank
    token = jnp.zeros((nt * MOE_TILE,), jnp.int32).at[dest].set(jnp.arange(2 * n, dtype=jnp.int32) // 2)
    past_end = (jnp.arange(nt, dtype=jnp.int32)[:, None] >= tile_end[None, :]).astype(jnp.int32)
    tile_expert = jnp.minimum(jnp.sum(past_end, axis=1), N_EXPERTS - 1).astype(jnp.int32)
    return token, dest.reshape(n, 2), tile_expert, tile_end[-1:].astype(jnp.int32)


def _moe(x1, w_router_pad, w_in, w_out):
    route = _router(x1, w_router_pad)
    token, pos, tile_expert, n_used = _moe_plan(route)
    xs = x1.astype(BF16)[token]
    ys = _moe_experts(xs, tile_expert, n_used, w_in, w_out)
    return ys[pos[:, 0]], ys[pos[:, 1]], route


def kernel(x_prompt, x_sample, cache_attn, state_hgrn, page_table, p_prompt, p_sample, w_in_even, w_out_even,
           w_in_odd, w_out_odd, hgrn_lower_bounds, hgrn_norm_w, ln_mix_g, ln_mix_b, ln_ffn_g, ln_ffn_b,
           w_ffn_in, w_ffn_out, w_router, w_moe_in, w_moe_out, w_ple_proj, w_ple_gate):
    batch, t, d = x_prompt.shape
    n_seq = x_sample.shape[0]
    n = batch * t
    past_len = page_table.shape[1] * PAGE_SIZE
    x = jnp.concatenate([x_prompt.reshape(n, d), x_sample.reshape(n_seq, d)], axis=0)
    pos = jnp.concatenate([jnp.tile(jnp.arange(t, dtype=jnp.int32), batch), jnp.full((n_seq,), past_len, jnp.int32)])
    cs = _rotary_table(pos)
    p_all = jnp.concatenate([p_prompt.reshape(DEPTH, n, PLE_DIM), p_sample.reshape(DEPTH, n_seq, PLE_DIM)], axis=1)
    join = lambda a, b: jnp.concatenate([a, b], axis=0)

    rows_p, rows_s, state_p, state_s = [], [], [], []
    for i in range(DEPTH):
        j = i // 2
        ple = (p_all[i], w_ple_gate[i].astype(BF16), w_ple_proj[i].astype(BF16))
        if i % 2 == 0:
            w_rows, w_q = _prep_even_in(w_in_even[j])
            rows, qa, qi, wi, qb, ka, va, ki, kb, vb = _even_inproj(x, w_rows, w_q, cs)
            oa_p = _dsa_prompt(qa, qi, wi, ka, va, ki, batch, t)
            ob_p = _sb_prompt(qb, kb, vb, batch, t)
            oa_s, ob_s = _even_decode(page_table, cache_attn, j, qa[n:], qi[n:], wi[n:], qb[n:], rows[n:])
            w_o = w_out_even[j].astype(BF16)
            x1 = _mix_ln(x, ln_mix_g[i], ln_mix_b[i],
                         mm=[(join(oa_p, oa_s), w_o[:WIDTH_A]), (join(ob_p, ob_s), w_o[WIDTH_A:])])
            h = _swiglu_in(x1, w_ffn_in[j].astype(BF16))
            x = _mix_ln(x1, ln_ffn_g[i], ln_ffn_b[i], mm=[(h, w_ffn_out[j].astype(BF16))], ple=ple)
            rows_p.append(rows[:n].reshape(batch, t, ROW_DIM))
            rows_s.append(rows[n:].reshape(n_seq, 1, ROW_DIM))
        else:
            h4 = _matmul(x, w_in_odd[j].astype(BF16))
            o_p, s_p = _hgrn_prompt(h4, hgrn_lower_bounds, hgrn_norm_w[j], batch, t, j)
            o_s, s_s = _hgrn_decode(h4[n:], state_hgrn[j], hgrn_lower_bounds, hgrn_norm_w[j], j)
            x1 = _mix_ln(x, ln_mix_g[i], ln_mix_b[i], mm=[(join(o_p, o_s), w_out_odd[j].astype(BF16))])
            w_r = jnp.pad(w_router[j], ((0, 0), (0, LANES - N_EXPERTS))).astype(BF16)
            y1, y2, route = _moe(x1, w_r, w_moe_in[j].astype(BF16), w_moe_out[j].astype(BF16))
            x = _mix_ln(x1, ln_ffn_g[i], ln_ffn_b[i], comb=(y1, y2), gates=route, ple=ple)
            state_p.append(s_p)
            state_s.append(s_s)
    return (x[:n].reshape(batch, t, d), x[n:].reshape(n_seq, 1, d), jnp.stack(rows_p), jnp.stack(rows_s),
            jnp.stack(state_p), jnp.stack(state_s))
```

```python
import functools

import jax
import jax.numpy as jnp
import numpy as np
from jax import lax
from jax.experimental import pallas as pl
from jax.experimental.pallas import tpu as pltpu

D_MODEL = 1024
DEPTH = 4
PAGE_SIZE = 128
HEAD_DIM = 64
N_HEADS_A = 8
N_HEADS_B = 8
IDX_HEADS = 4
IDX_DIM = 64
TOPK_MAX = 256
ROPE_THETA = 500000.0
H_C = 8
HGRN_DK = 128
HGRN_DV = D_MODEL // H_C
D_FF = 2816
N_EXPERTS = 8
D_FF_EXPERT = 3584
PLE_DIM = 256
DEEPNORM_ALPHA = (2 * DEPTH) ** 0.25
LN_EPS = 1e-5
RMS_EPS = 1e-6
NEG_BIG = -1e30
WIDTH_A = N_HEADS_A * HEAD_DIM
WIDTH_B = N_HEADS_B * HEAD_DIM
ROW_DIM = 2 * WIDTH_A + IDX_DIM + 2 * WIDTH_B

LANES = 128
HALF = LANES // 2
VMEM_LIMIT = 56 * 1024 * 1024
TOKEN_TILE = 384
INT_MIN = -2 ** 31
NO_INDEX = 2 ** 30

BF16 = jnp.bfloat16
F32 = jnp.float32


def _cparams(*sem, vmem=VMEM_LIMIT):
    return pltpu.CompilerParams(dimension_semantics=sem, vmem_limit_bytes=vmem)


def _dot(a, b):
    return jnp.dot(a, b, preferred_element_type=F32)


def _dot_nt(a, b):
    return lax.dot_general(a, b, (((1,), (1,)), ((), ())), preferred_element_type=F32)


def _dot_tn(a, b):
    return lax.dot_general(a, b, (((0,), (0,)), ((), ())), preferred_element_type=F32)


def _low_half(shape):
    return lax.broadcasted_iota(jnp.int32, shape, len(shape) - 1) < HALF


def _split_hi_lo(x):
    hi = x.astype(BF16)
    lo = (x - hi.astype(F32)).astype(BF16)
    return hi, lo


def _rotary(t, c, s1, s2):
    return t * c + pltpu.roll(t, LANES - 8, 1) * s1 + pltpu.roll(t, 8, 1) * s2


def _even_inproj_kernel(x_ref, wr_ref, wq_ref, cs_ref, rows_ref, qa_ref, qi_ref, wi_ref, qb_ref,
                        ka_ref, va_ref, ki_ref, kb_ref, vb_ref):
    xb = x_ref[...].astype(BF16)
    c = cs_ref[:, 0:LANES]
    s1 = cs_ref[:, LANES:2 * LANES]
    s2 = cs_ref[:, 2 * LANES:3 * LANES]
    low = _low_half(c.shape)
    acc = _dot(xb, wr_ref[...])
    for t in range(WIDTH_A // LANES):
        sl = slice(t * LANES, (t + 1) * LANES)
        r = _rotary(acc[:, sl], c, s1, s2)
        rows_ref[:, sl] = r
        ka_ref[:, sl] = r.astype(BF16)
    v_a = acc[:, WIDTH_A:2 * WIDTH_A]
    rows_ref[:, WIDTH_A:2 * WIDTH_A] = v_a
    va_ref[...] = v_a.astype(BF16)
    o_i = 2 * WIDTH_A
    t8 = acc[:, o_i:o_i + LANES]
    r8 = _rotary(t8, jnp.where(low, c, 1.0), jnp.where(low, s1, 0.0), jnp.where(low, s2, 0.0))
    rows_ref[:, o_i:o_i + LANES] = r8
    ki_ref[...] = jnp.where(low, r8, pltpu.roll(r8, HALF, 1)).astype(BF16)
    rows_ref[:, o_i + LANES:ROW_DIM] = acc[:, o_i + LANES:ROW_DIM]
    o_b = o_i + IDX_DIM
    kb_ref[...] = acc[:, o_b:o_b + WIDTH_B].astype(BF16)
    vb_ref[...] = acc[:, o_b + WIDTH_B:o_b + 2 * WIDTH_B].astype(BF16)

    accq = _dot(xb, wq_ref[...])
    scale = HEAD_DIM ** -0.5

    def put_pairs(dst_ref, src_off, n_pairs, rot):
        for p in range(n_pairs):
            t = accq[:, src_off + p * LANES:src_off + (p + 1) * LANES]
            if rot:
                t = _rotary(t, c, s1, s2)
            t = t * scale
            dst_ref[:, (2 * p) * LANES:(2 * p + 1) * LANES] = jnp.where(low, t, 0.0).astype(BF16)
            dst_ref[:, (2 * p + 1) * LANES:(2 * p + 2) * LANES] = jnp.where(low, 0.0, t).astype(BF16)

    put_pairs(qa_ref, 0, N_HEADS_A // 2, True)
    put_pairs(qi_ref, WIDTH_A, IDX_HEADS // 2, True)
    put_pairs(qb_ref, WIDTH_A + IDX_HEADS * IDX_DIM, N_HEADS_B // 2, False)
    o_w = WIDTH_A + IDX_HEADS * IDX_DIM + WIDTH_B
    wi_ref[...] = accq[:, o_w:o_w + LANES] * IDX_HEADS ** -0.5


def _even_inproj(x, w_rows, w_q, cs, tm=TOKEN_TILE):
    n = x.shape[0]
    qw = w_q.shape[1]
    row = lambda i: (i, 0)
    const = lambda i: (0, 0)
    outs = [
        (ROW_DIM, F32), (2 * WIDTH_A, BF16), (2 * IDX_HEADS * IDX_DIM, BF16), (LANES, F32), (2 * WIDTH_B, BF16),
        (WIDTH_A, BF16), (WIDTH_A, BF16), (LANES, BF16), (WIDTH_B, BF16), (WIDTH_B, BF16),
    ]
    return pl.pallas_call(
        _even_inproj_kernel,
        grid=(pl.cdiv(n, tm),),
        in_specs=[pl.BlockSpec((tm, D_MODEL), row), pl.BlockSpec((D_MODEL, ROW_DIM), const),
                  pl.BlockSpec((D_MODEL, qw), const), pl.BlockSpec((tm, 3 * LANES), row)],
        out_specs=[pl.BlockSpec((tm, w), row) for w, _ in outs],
        out_shape=[jax.ShapeDtypeStruct((n, w), d) for w, d in outs],
        compiler_params=_cparams("parallel"),
        name="even_inproj",
    )(x, w_rows, w_q, cs)


def _matmul_kernel(x_ref, w_ref, o_ref):
    o_ref[...] = _dot(x_ref[...].astype(BF16), w_ref[...]).astype(o_ref.dtype)


def _matmul(x, w, out_dtype=F32, tm=TOKEN_TILE, tn=1024):
    n, k = x.shape
    m = w.shape[1]
    tn = min(tn, m)
    return pl.pallas_call(
        _matmul_kernel,
        grid=(pl.cdiv(n, tm), m // tn),
        in_specs=[pl.BlockSpec((tm, k), lambda i, j: (i, 0)), pl.BlockSpec((k, tn), lambda i, j: (0, j))],
        out_specs=pl.BlockSpec((tm, tn), lambda i, j: (i, j)),
        out_shape=jax.ShapeDtypeStruct((n, m), out_dtype),
        compiler_params=_cparams("parallel", "arbitrary"),
        name="matmul",
    )(x, w)


def _swiglu_in_kernel(x_ref, wa_ref, wb_ref, o_ref):
    xb = x_ref[...].astype(BF16)
    a = _dot(xb, wa_ref[...])
    b = _dot(xb, wb_ref[...])
    o_ref[...] = (a * jax.nn.sigmoid(a) * b).astype(o_ref.dtype)


def _swiglu_in(x, w_in, tm=TOKEN_TILE, tn=1408):
    n, k = x.shape
    f = w_in.shape[1] // 2
    nj = f // tn
    return pl.pallas_call(
        _swiglu_in_kernel,
        grid=(pl.cdiv(n, tm), nj),
        in_specs=[pl.BlockSpec((tm, k), lambda i, j: (i, 0)),
                  pl.BlockSpec((k, tn), lambda i, j: (0, j)),
                  pl.BlockSpec((k, tn), lambda i, j: (0, j + nj))],
        out_specs=pl.BlockSpec((tm, tn), lambda i, j: (i, j)),
        out_shape=jax.ShapeDtypeStruct((n, f), BF16),
        compiler_params=_cparams("parallel", "arbitrary"),
        name="swiglu_in",
    )(x, w_in, w_in)


def _layernorm(y, g, b):
    mu = jnp.mean(y, axis=-1, keepdims=True)
    d = y - mu
    var = jnp.mean(d * d, axis=-1, keepdims=True)
    return d * lax.rsqrt(var + LN_EPS) * g + b


def _mix_ln_kernel(n_mm, n_comb, ple, *refs):
    it = iter(refs)
    mm = [(next(it), next(it)) for _ in range(n_mm)]
    comb = [next(it) for _ in range(n_comb)]
    gates_ref = next(it) if n_comb else None
    x_ref, g_ref, b_ref = next(it), next(it), next(it)
    if ple:
        p_ref, wg_ref, wp_ref = next(it), next(it), next(it)
    o_ref = next(it)
    y = DEEPNORM_ALPHA * x_ref[...]
    for h_ref, w_ref in mm:
        y = y + _dot(h_ref[...], w_ref[...])
    for k, y_ref in enumerate(comb):
        y = y + gates_ref[:, k:k + 1] * y_ref[...]
    y = _layernorm(y, g_ref[...], b_ref[...])
    if ple:
        gate = jax.nn.sigmoid(_dot(y.astype(BF16), wg_ref[...]))
        y = y + gate * _dot(p_ref[...].astype(BF16), wp_ref[...])
    o_ref[...] = y


def _mix_ln(x, g, b, mm=(), comb=(), gates=None, ple=None, tm=TOKEN_TILE):
    n = x.shape[0]
    row = lambda i: (i, 0)
    const = lambda i: (0, 0)
    args, specs = [], []
    for h, w in mm:
        args += [h, w]
        specs += [pl.BlockSpec((tm, h.shape[1]), row), pl.BlockSpec(w.shape, const)]
    for y in comb:
        args.append(y)
        specs.append(pl.BlockSpec((tm, D_MODEL), row))
    if comb:
        args.append(gates)
        specs.append(pl.BlockSpec((tm, gates.shape[1]), row))
    args += [x, g.reshape(1, D_MODEL), b.reshape(1, D_MODEL)]
    specs += [pl.BlockSpec((tm, D_MODEL), row), pl.BlockSpec((1, D_MODEL), const), pl.BlockSpec((1, D_MODEL), const)]
    if ple is not None:
        p, wg, wp = ple
        args += [p, wg, wp]
        specs += [pl.BlockSpec((tm, PLE_DIM), row), pl.BlockSpec(wg.shape, const), pl.BlockSpec(wp.shape, const)]
    return pl.pallas_call(
        functools.partial(_mix_ln_kernel, len(mm), len(comb), ple is not None),
        grid=(pl.cdiv(n, tm),),
        in_specs=specs,
        out_specs=pl.BlockSpec((tm, D_MODEL), row),
        out_shape=jax.ShapeDtypeStruct((n, D_MODEL), F32),
        compiler_params=_cparams("parallel"),
        name="mix_ln",
    )(*args)


def _prep_even_in(w):
    sizes = (WIDTH_A, WIDTH_A, WIDTH_A, IDX_HEADS * IDX_DIM, IDX_DIM, IDX_HEADS, WIDTH_B, WIDTH_B, WIDTH_B)
    qa, ka, va, qi, ki, wi, qb, kb, vb = jnp.split(w, np.cumsum(sizes)[:-1].tolist(), axis=1)
    w_rows = jnp.concatenate([ka, va, ki, kb, vb], axis=1)
    pad = jnp.zeros((w.shape[0], LANES - IDX_HEADS), w.dtype)
    w_q = jnp.concatenate([qa, qi, qb, wi, pad], axis=1)
    return w_rows.astype(BF16), w_q.astype(BF16)


def _rotary_table(pos):
    rd = HEAD_DIM // 4
    half = rd // 2
    inv_freq = ROPE_THETA ** (-jnp.arange(half, dtype=F32) * 2.0 / rd)
    ang = pos.astype(F32)[:, None] * inv_freq[None, :]
    cos, sin = jnp.cos(ang), jnp.sin(ang)
    n = pos.shape[0]
    ones = jnp.ones((n, HEAD_DIM - rd), F32)
    zeros = jnp.zeros((n, HEAD_DIM - rd), F32)
    zh = jnp.zeros((n, half), F32)
    c = jnp.concatenate([cos, cos, ones], axis=1)
    s1 = jnp.concatenate([-sin, zh, zeros], axis=1)
    s2 = jnp.concatenate([zh, sin, zeros], axis=1)
    return jnp.concatenate([c, c, s1, s1, s2, s2], axis=1)


def _softplus(z):
    return jnp.maximum(z, 0.0) + jnp.log(1.0 + jnp.exp(-jnp.abs(z)))


def _sb_tiles(qs, kvs, upper2, state, valid):
    work = [(c, kt, vt) for kt, vt in kvs for c in range(len(qs))]
    zs = [_dot_nt(qs[c], kt) for c, kt, _ in work]
    sps = [_softplus(z) for z in zs]
    sp_ms = sps if valid is None else [jnp.where(valid, sp, 0.0) for sp in sps]
    sums = [_dot(jnp.concatenate(_split_hi_lo(sp_m), axis=1), upper2) for sp_m in sp_ms]
    carries = [carry for carry, _ in state]
    ws = []
    for (c, _, _), z, sp, sp_m, x in zip(work, zs, sps, sp_ms, sums):
        w = jnp.exp(z - sp + (carries[c] - x))
        ws.append((w if valid is None else jnp.where(valid, w, 0.0)).astype(BF16))
        carries[c] = carries[c] - jnp.sum(sp_m, axis=1, keepdims=True)
    accs = [acc for _, acc in state]
    for (c, _, vt), w in zip(work, ws):
        accs[c] = accs[c] + _dot(w, vt)
    return tuple(zip(carries, accs))


def _sb_prompt_kernel(q_ref, k_ref, v_ref, o_ref, *, tq, unroll):
    qi = pl.program_id(2)
    rows = lax.broadcasted_iota(jnp.int32, (tq, tq), 0)
    cols = lax.broadcasted_iota(jnp.int32, (tq, tq), 1)
    upper = (rows > cols).astype(BF16)
    upper2 = jnp.concatenate([upper, upper], axis=0)
    qs = [q_ref[:, 0:LANES], q_ref[:, LANES:2 * LANES]]

    def tiles(j_first, n, state, valid):
        kvs = []
        for u in range(n):
            start = pl.multiple_of((j_first - u) * tq, tq)
            kvs.append((k_ref[pl.ds(start, tq), :], v_ref[pl.ds(start, tq), :]))
        return _sb_tiles(qs, kvs, upper2, state, valid)

    zero = (jnp.zeros((tq, 1), F32), jnp.zeros((tq, LANES), F32))
    state = tiles(qi, 1, (zero, zero), cols < rows)
    n_left = qi % unroll
    for u in range(1, unroll):
        state = lax.cond(n_left >= u, lambda st, u=u: tiles(qi - u, 1, st, None), lambda st: st, state)
    first = qi - 1 - n_left
    state = lax.fori_loop(0, qi // unroll, lambda jj, st: tiles(first - unroll * jj, unroll, st, None), state)
    low = _low_half((tq, LANES))
    o_ref[...] = jnp.where(low, state[0][1], state[1][1]).astype(o_ref.dtype)


def _sb_prompt(q_pad, k, v, batch, t, tq=256, unroll=2):
    tq = min(tq, t)
    nq = t // tq
    return pl.pallas_call(
        functools.partial(_sb_prompt_kernel, tq=tq, unroll=unroll),
        grid=(batch, N_HEADS_B // 2, nq),
        in_specs=[pl.BlockSpec((tq, 2 * LANES), lambda b, p, i: (b * nq + i, p)),
                  pl.BlockSpec((t, LANES), lambda b, p, i: (b, p)),
                  pl.BlockSpec((t, LANES), lambda b, p, i: (b, p))],
        out_specs=pl.BlockSpec((tq, LANES), lambda b, p, i: (b * nq + i, p)),
        out_shape=jax.ShapeDtypeStruct((batch * t, WIDTH_B), BF16),
        compiler_params=_cparams("parallel", "parallel", "arbitrary"),
        name="sb_prompt",
    )(q_pad, k, v)


def _sortable(x):
    b = lax.bitcast_convert_type(x + 0.0, jnp.int32)
    return b ^ ((b >> 31) & 0x7FFFFFFF)


def _dsa_prompt_kernel(qa_ref, qi_ref, wi_ref, ka_ref, va_ref, ki_ref, o_ref, key_sc, bias_sc, mx_sc, acc_sc,
                       *, tq, tk, n_sel, t):
    q0 = pl.program_id(1) * tq
    nkt = (q0 + tq + tk - 1) // tk
    rows = q0 + lax.broadcasted_iota(jnp.int32, (tq, tk), 0)
    cols0 = lax.broadcasted_iota(jnp.int32, (tq, tk), 1)
    w = wi_ref[...]

    def score_tile(j, carry):
        start = pl.multiple_of(j * tk, tk)
        kt = ki_ref[pl.ds(start, tk), :]
        ss = [_dot_nt(qi_ref[:, h * LANES:(h + 1) * LANES], kt) for h in range(IDX_HEADS)]
        score = jnp.zeros((tq, tk), F32)
        for h in range(IDX_HEADS):
            score = score + jnp.maximum(ss[h], 0.0) * w[:, h:h + 1]
        score = jnp.where(start + cols0 <= rows, score, NEG_BIG)
        key_sc[:, pl.ds(start, tk)] = _sortable(score)
        return carry

    lax.fori_loop(0, nkt, score_tile, 0)

    def count(pred):
        def body(j, c):
            start = pl.multiple_of(j * tk, tk)
            hit = jnp.where(pred(key_sc[:, pl.ds(start, tk)], start + cols0), 1.0, 0.0)
            for u in range(tk // LANES):
                c = c + hit[:, u * LANES:(u + 1) * LANES]
            return c
        c = lax.fori_loop(0, nkt, body, jnp.zeros((tq, LANES), F32))
        return jnp.sum(c, axis=1, keepdims=True)

    def bit_step(i, v):
        cand = v ^ lax.shift_left(jnp.int32(1), 31 - i)
        return jnp.where(count(lambda kk, kp: kk >= cand) >= n_sel, cand, v)

    v = lax.fori_loop(0, 32, bit_step, jnp.full((tq, 1), INT_MIN, jnp.int32))
    r = n_sel - count(lambda kk, kp: kk > v)
    n_eq = count(lambda kk, kp: kk == v)

    def tie_cut():
        nbits = (t - 1).bit_length()

        def step(i, c):
            cand = c + lax.shift_left(jnp.int32(1), nbits - 1 - i)
            g = count(lambda kk, kp: jnp.where(kk == v, kp, NO_INDEX) < cand)
            return jnp.where(g < r, cand, c)

        return lax.fori_loop(0, nbits, step, jnp.zeros((tq, 1), jnp.int32))

    cut = lax.cond(jnp.max(n_eq - r) > 0, tie_cut, lambda: jnp.full((tq, 1), t, jnp.int32))

    heads = range(N_HEADS_A)

    def logits(start):
        kts = [ka_ref[pl.ds(start, tk), p * LANES:(p + 1) * LANES] for p in range(N_HEADS_A // 2)]
        return [_dot_nt(qa_ref[:, h * LANES:(h + 1) * LANES], kts[h // 2]) for h in heads]

    mx_sc[...] = jnp.full(mx_sc.shape, NEG_BIG, F32)

    def max_tile(j, carry):
        start = pl.multiple_of(j * tk, tk)
        kk = key_sc[:, pl.ds(start, tk)]
        kp = start + cols0
        tie_ok = jnp.where(kk == v, kp, NO_INDEX) <= cut
        bias = jnp.where(kp <= rows, jnp.where(kk > v, 0.0, jnp.where(tie_ok, 0.0, NEG_BIG)), NEG_BIG)
        bias_sc[:, pl.ds(start, tk)] = bias
        for h, s in zip(heads, logits(start)):
            mx_sc[h] = jnp.maximum(mx_sc[h], s + bias)
        return carry

    lax.fori_loop(0, nkt, max_tile, 0)
    m = [jnp.max(mx_sc[h], axis=1, keepdims=True) for h in heads]
    mx_sc[...] = jnp.zeros(mx_sc.shape, F32)
    acc_sc[...] = jnp.zeros(acc_sc.shape, F32)

    def sum_tile(j, carry):
        start = pl.multiple_of(j * tk, tk)
        bias = bias_sc[:, pl.ds(start, tk)]
        vts = [va_ref[pl.ds(start, tk), p * LANES:(p + 1) * LANES] for p in range(N_HEADS_A // 2)]
        prs = []
        for h, s in zip(heads, logits(start)):
            pr = jnp.exp(s + bias - m[h])
            mx_sc[h] = mx_sc[h] + pr
            prs.append(pr.astype(BF16))
        for h in heads:
            acc_sc[h] = acc_sc[h] + _dot(prs[h], vts[h // 2])
        return carry

    lax.fori_loop(0, nkt, sum_tile, 0)
    low = _low_half((tq, LANES))
    outs = [acc_sc[h] / jnp.sum(mx_sc[h], axis=1, keepdims=True) for h in heads]
    for p in range(N_HEADS_A // 2):
        o_ref[:, p * LANES:(p + 1) * LANES] = jnp.where(low, outs[2 * p], outs[2 * p + 1]).astype(o_ref.dtype)


def _dsa_prompt(qa_pad, qi_pad, wi, ka, va, ki2, batch, t, tq=128, tk=512):
    tq = min(tq, t)
    tk = min(tk, t)
    nq = t // tq
    n_sel = max(1, min(TOPK_MAX, t // 4))
    qrow = lambda b, i: (b * nq + i, 0)
    seq = lambda b, i: (b, 0)
    return pl.pallas_call(
        functools.partial(_dsa_prompt_kernel, tq=tq, tk=tk, n_sel=n_sel, t=t),
        grid=(batch, nq),
        in_specs=[pl.BlockSpec((tq, 2 * WIDTH_A), qrow), pl.BlockSpec((tq, 2 * IDX_HEADS * IDX_DIM), qrow),
                  pl.BlockSpec((tq, LANES), qrow), pl.BlockSpec((t, WIDTH_A), seq), pl.BlockSpec((t, WIDTH_A), seq),
                  pl.BlockSpec((t, LANES), seq)],
        out_specs=pl.BlockSpec((tq, WIDTH_A), qrow),
        out_shape=jax.ShapeDtypeStruct((batch * t, WIDTH_A), BF16),
        scratch_shapes=[pltpu.VMEM((tq, t), jnp.int32), pltpu.VMEM((tq, t), F32),
                        pltpu.VMEM((N_HEADS_A, tq, tk), F32), pltpu.VMEM((N_HEADS_A, tq, LANES), F32)],
        compiler_params=_cparams("parallel", "arbitrary"),
        name="dsa_prompt",
    )(qa_pad, qi_pad, wi, ka, va, ki2)


OFF_KI = 2 * WIDTH_A
OFF_KB = OFF_KI + IDX_DIM
OFF_VB = OFF_KB + WIDTH_B


def _head_rows(row, off, n_rows=8):
    w = row.shape[1]
    col = lax.broadcasted_iota(jnp.int32, (n_rows, w), 1)
    hrow = lax.broadcasted_iota(jnp.int32, (n_rows, w), 0)
    head_of_col = ((col - off + HEAD_DIM) >> 6) - 1
    return jnp.where(head_of_col == hrow, jnp.broadcast_to(row, (n_rows, w)), 0.0)


def _sum_all(x):
    return jnp.sum(jnp.sum(x, axis=0, keepdims=True), axis=1, keepdims=True)


def _even_decode_kernel(pt_ref, qa_ref, qi_ref, wi_ref, qb_ref, self_ref, *rest, n_pages, n_sel):
    pages = rest[:n_pages]
    oa_ref, ob_ref, key_sc, la_sc, z_sc = rest[n_pages:]
    n_past = n_pages * PAGE_SIZE
    f32_tile = lambda ref, h: ref[:, h * LANES:(h + 1) * LANES].astype(F32)
    swap = lambda t: pltpu.roll(t, HALF, 1)

    compact = lambda ref, n_heads: jnp.concatenate(
        [f32_tile(ref, 2 * p) + f32_tile(ref, 2 * p + 1) for p in range(n_heads // 2)], axis=1)
    qa_bd16 = _head_rows(compact(qa_ref, N_HEADS_A), 0).astype(BF16)
    qb_bd16 = _head_rows(compact(qb_ref, N_HEADS_B), 0).astype(BF16)
    head_mask = _head_rows(jnp.ones((1, WIDTH_A), F32), 0)
    page16 = lambda s, off, width: pages[s][off:off + width, :].astype(BF16)
    hrow = lax.broadcasted_iota(jnp.int32, (8, LANES), 0)
    lane8 = lax.broadcasted_iota(jnp.int32, (8, LANES), 1)
    qi_rows = jnp.zeros((8, LANES), F32)
    for h in range(IDX_HEADS):
        t = f32_tile(qi_ref, h)
        t = swap(t) if h % 2 else t
        qi_rows = jnp.where(hrow == h, jnp.broadcast_to(t, (8, LANES)), qi_rows)
    qi_rows16 = qi_rows.astype(BF16)
    w_col = jnp.sum(jnp.where(lane8 == hrow, jnp.broadcast_to(wi_ref[...], (8, LANES)), 0.0), axis=1, keepdims=True)

    def idx_score(s_i):
        return jnp.sum(jnp.maximum(s_i, 0.0) * w_col, axis=0, keepdims=True)

    for s in range(n_pages):
        key_sc[s:s + 1, :] = _sortable(idx_score(_dot(qi_rows16, page16(s, OFF_KI, LANES))))
        la_sc[s] = _dot(qa_bd16, page16(s, 0, WIDTH_A))
        z_sc[s] = _dot(qb_bd16, page16(s, OFF_KB, WIDTH_B))

    rnd = lambda x: x.astype(BF16).astype(F32)
    s_self = jnp.sum(qi_rows16.astype(F32) * rnd(self_ref[:, OFF_KI:OFF_KI + LANES]), axis=1, keepdims=True)
    sc_self = idx_score(s_self)
    la_self = jnp.sum(qa_bd16.astype(F32) * rnd(self_ref[:, 0:WIDTH_A]), axis=1, keepdims=True)
    lane1 = lax.broadcasted_iota(jnp.int32, (1, LANES), 1)
    key_sc[n_pages:n_pages + 1, :] = jnp.where(lane1 == 0, _sortable(jnp.broadcast_to(sc_self, (1, LANES))), INT_MIN)

    keys = key_sc[...]
    kp = (lax.broadcasted_iota(jnp.int32, keys.shape, 0) * PAGE_SIZE
          + lax.broadcasted_iota(jnp.int32, keys.shape, 1))
    count = lambda pred: _sum_all(jnp.where(pred, 1, 0))

    def bit_step(i, v):
        cand = v ^ lax.shift_left(jnp.int32(1), 31 - i)
        return jnp.where(count(keys >= cand) >= n_sel, cand, v)

    v = lax.fori_loop(0, 32, bit_step, jnp.full((1, 1), INT_MIN, jnp.int32))
    r = n_sel - count(keys > v)
    n_eq = count(keys == v)
    tie_pos = jnp.where(keys == v, kp, NO_INDEX)

    def tie_cut():
        nbits = n_past.bit_length()

        def step(i, c):
            cand = c + lax.shift_left(jnp.int32(1), nbits - 1 - i)
            return jnp.where(count(tie_pos < cand) < r, cand, c)

        return lax.fori_loop(0, nbits, step, jnp.zeros((1, 1), jnp.int32))

    cut = lax.cond(jnp.max(n_eq - r) > 0, tie_cut, lambda: jnp.full((1, 1), NO_INDEX - 1, jnp.int32))
    bias = jnp.where(kp <= n_past, jnp.where(keys > v, 0.0, jnp.where(tie_pos <= cut, 0.0, NEG_BIG)), NEG_BIG)

    bias_self = bias[n_pages:n_pages + 1, 0:1]
    m = la_self + bias_self
    for s in range(n_pages):
        m = jnp.maximum(m, jnp.max(la_sc[s] + bias[s:s + 1, :], axis=1, keepdims=True))
    p_self = jnp.exp(la_self + bias_self - m)
    l = p_self
    acc = rnd(p_self) * rnd(self_ref[:, WIDTH_A:2 * WIDTH_A])
    for s in range(n_pages):
        pr = jnp.exp(la_sc[s] + bias[s:s + 1, :] - m)
        l = l + jnp.sum(pr, axis=1, keepdims=True)
        acc = acc + _dot_nt(pr.astype(BF16), page16(s, WIDTH_A, WIDTH_A))
    oa_ref[...] = jnp.sum(head_mask * (acc / l), axis=0, keepdims=True).astype(oa_ref.dtype)

    t_idx = lax.broadcasted_iota(jnp.int32, (PAGE_SIZE, PAGE_SIZE), 0)
    s_idx = lax.broadcasted_iota(jnp.int32, (PAGE_SIZE, PAGE_SIZE), 1)
    upper = (t_idx > s_idx).astype(BF16)
    carry = jnp.zeros((8, 1), F32)
    acc_b = jnp.zeros((8, WIDTH_B), F32)
    for s in reversed(range(n_pages)):
        z = z_sc[s]
        sp = _softplus(z)
        hi, lo = _split_hi_lo(-sp)
        later = _dot(hi, upper) + _dot(lo, upper) + carry
        w = jnp.exp(z - sp + later)
        acc_b = acc_b + _dot_nt(w.astype(BF16), page16(s, OFF_VB, WIDTH_B))
        carry = carry - jnp.sum(sp, axis=1, keepdims=True)
    ob_ref[...] = jnp.sum(head_mask * acc_b, axis=0, keepdims=True).astype(ob_ref.dtype)


def _even_decode(page_table, cache_t, layer, qa_pad, qi_pad, wi, qb_pad, self_rows):
    n, n_pages = page_table.shape
    n_sel = max(1, min(TOPK_MAX, (n_pages * PAGE_SIZE + 1) // 4))
    per_seq = lambda w: pl.BlockSpec((None, 1, w), lambda b, pt: (b, 0, 0))
    page_spec = lambda s: pl.BlockSpec((None, None, ROW_DIM, PAGE_SIZE), lambda b, pt: (layer, pt[b, s], 0, 0))
    r3 = lambda a: a.reshape(n, 1, a.shape[-1])
    oa, ob = pl.pallas_call(
        functools.partial(_even_decode_kernel, n_pages=n_pages, n_sel=n_sel),
        grid_spec=pltpu.PrefetchScalarGridSpec(
            num_scalar_prefetch=1, grid=(n,),
            in_specs=[per_seq(2 * WIDTH_A), per_seq(2 * IDX_HEADS * IDX_DIM), per_seq(LANES), per_seq(2 * WIDTH_B),
                      per_seq(ROW_DIM)] + [page_spec(s) for s in range(n_pages)],
            out_specs=[per_seq(WIDTH_A), per_seq(WIDTH_B)],
            scratch_shapes=[pltpu.VMEM((n_pages + 1, LANES), jnp.int32), pltpu.VMEM((n_pages, 8, LANES), F32),
                            pltpu.VMEM((n_pages, 8, LANES), F32)]),
        out_shape=[jax.ShapeDtypeStruct((n, 1, WIDTH_A), BF16), jax.ShapeDtypeStruct((n, 1, WIDTH_B), BF16)],
        compiler_params=_cparams("parallel"),
        name="even_decode",
    )(page_table, r3(qa_pad), r3(qi_pad), r3(wi), r3(qb_pad), r3(self_rows), *([cache_t] * n_pages))
    return oa.reshape(n, WIDTH_A), ob.reshape(n, WIDTH_B)


def _hgrn_lower_bound(lb_ref, layer):
    x = lb_ref[...]
    e = jnp.exp(x - jnp.max(x, axis=0, keepdims=True))
    sm = e / jnp.sum(e, axis=0, keepdims=True)
    lb = jnp.zeros((1, sm.shape[1]), F32)
    for r in range(1, layer + 1):
        lb = lb + sm[r:r + 1, :]
    return lb


def _hgrn_gates(qr, fl, lb):
    q = qr * jax.nn.sigmoid(qr) * HGRN_DK ** -0.5
    log_f = jnp.log(lb + (1.0 - lb) * jax.nn.sigmoid(fl))
    k = (1.0 - lb) * jax.nn.sigmoid(-fl)
    return q, k, log_f


def _hgrn_out(o, gr, nw):
    o = o * lax.rsqrt(jnp.mean(o * o, axis=-1, keepdims=True) + RMS_EPS) * nw
    return o * (gr * jax.nn.sigmoid(gr))


def _hgrn_prompt_kernel(q_ref, f_ref, i_ref, g_ref, lb_ref, nw_ref, o_ref, s_ref, st_sc, *, c, m, hp, layer):
    ci = pl.program_id(2)

    @pl.when(ci == 0)
    def _():
        st_sc[...] = jnp.zeros(st_sc.shape, F32)

    heads = range(hp)
    hs = lambda ref, h: ref[:, h * HGRN_DK:(h + 1) * HGRN_DK]
    lb_all = _hgrn_lower_bound(lb_ref, layer)
    gates = [_hgrn_gates(hs(q_ref, h), hs(f_ref, h), hs(lb_all, h)) for h in heads]
    q = [g[0] for g in gates]
    k = [g[1] for g in gates]
    v = [hs(i_ref, h) for h in heads]
    vb = [x.astype(BF16) for x in v]
    t_idx = lax.broadcasted_iota(jnp.int32, (c, c), 0)
    s_idx = lax.broadcasted_iota(jnp.int32, (c, c), 1)
    tril = (s_idx <= t_idx).astype(BF16)
    tril2 = jnp.concatenate([tril, tril], axis=1)
    G = [_dot(tril2, jnp.concatenate(_split_hi_lo(g[2]), axis=0)) for g in gates]

    st = [st_sc[h] for h in heads]
    o = [_dot_nt((q[h] * jnp.exp(G[h])).astype(BF16), st[h].astype(BF16)) for h in heads]

    pieces = [[jnp.zeros((m, HGRN_DV), F32)] for _ in heads]
    for blk in range(1, c // m):
        lo_r, hi_r = blk * m, (blk + 1) * m
        scs = []
        for h in heads:
            g_row = G[h][lo_r - 1:lo_r, :]
            qt = q[h][lo_r:hi_r] * jnp.exp(G[h][lo_r:hi_r] - g_row)
            kt = k[h][0:lo_r] * jnp.exp(g_row - G[h][0:lo_r])
            scs.append(_dot_nt(qt.astype(BF16), kt.astype(BF16)))
        for h in heads:
            pieces[h].append(_dot(scs[h].astype(BF16), vb[h][0:lo_r]))
    o = [o[h] + jnp.concatenate(pieces[h], axis=0) for h in heads]

    st_new = []
    for h in heads:
        g_end = G[h][c - 1:c, :]
        kd = k[h] * jnp.exp(g_end - G[h])
        st_new.append(st[h] * jnp.exp(g_end) + _dot_tn(vb[h], kd.astype(BF16)))
    for h in heads:
        st_sc[h] = st_new[h]

    in_blk = lax.broadcasted_iota(jnp.int32, (c, 1), 0) % m
    for d in range(m):
        for h in heads:
            ks = k[h] if d == 0 else pltpu.roll(k[h], d, 0)
            gs = G[h] if d == 0 else pltpu.roll(G[h], d, 0)
            vs = v[h] if d == 0 else pltpu.roll(v[h], d, 0)
            a = jnp.sum(q[h] * ks * jnp.exp(jnp.minimum(G[h] - gs, 0.0)), axis=1, keepdims=True)
            o[h] = o[h] + jnp.where(in_blk >= d, a, 0.0) * vs

    nw = nw_ref[...]
    for h in heads:
        o_ref[:, h * HGRN_DV:(h + 1) * HGRN_DV] = _hgrn_out(o[h], hs(g_ref, h), nw).astype(o_ref.dtype)

    @pl.when(ci == pl.num_programs(2) - 1)
    def _():
        for h in heads:
            s_ref[h] = st_new[h].T


def _hgrn_prompt(h4, lbs_raw, norm_w, batch, t, layer, c=128, m=16, hp=H_C):
    c = min(c, t)
    nc = t // c
    ng = H_C // hp
    w = hp * HGRN_DK
    col = lambda part: (lambda b, g, i: (b * nc + i, part * ng + g))
    return pl.pallas_call(
        functools.partial(_hgrn_prompt_kernel, c=c, m=m, hp=hp, layer=layer),
        grid=(batch, ng, nc),
        in_specs=[pl.BlockSpec((c, w), col(0)), pl.BlockSpec((c, w), col(1)),
                  pl.BlockSpec((c, w), col(2)), pl.BlockSpec((c, w), col(3)),
                  pl.BlockSpec((lbs_raw.shape[0], w), lambda b, g, i: (0, g)),
                  pl.BlockSpec((1, HGRN_DV), lambda b, g, i: (0, 0))],
        out_specs=[pl.BlockSpec((c, w), lambda b, g, i: (b * nc + i, g)),
                   pl.BlockSpec((None, hp, HGRN_DK, HGRN_DV), lambda b, g, i: (b, g, 0, 0))],
        out_shape=[jax.ShapeDtypeStruct((batch * t, D_MODEL), BF16),
                   jax.ShapeDtypeStruct((batch, H_C, HGRN_DK, HGRN_DV), F32)],
        scratch_shapes=[pltpu.VMEM((hp, HGRN_DV, HGRN_DK), F32)],
        compiler_params=_cparams("parallel", "parallel", "arbitrary"),
        name="hgrn_prompt",
    )(h4, h4, h4, h4, lbs_raw, norm_w.reshape(1, HGRN_DV))


def _lane_to_sublane(row):
    n = row.shape[1]
    eye = lax.broadcasted_iota(jnp.int32, (n, n), 0) == lax.broadcasted_iota(jnp.int32, (n, n), 1)
    return jnp.sum(jnp.where(eye, jnp.broadcast_to(row, (n, n)), 0.0), axis=1, keepdims=True)


def _hgrn_decode_kernel(h_ref, s0_ref, lb_ref, nw_ref, o_ref, s_ref, *, layer):
    lb_all = _hgrn_lower_bound(lb_ref, layer)
    nw = nw_ref[...]
    for h in range(H_C):
        sl = lambda part: slice((part * H_C + h) * HGRN_DK, (part * H_C + h + 1) * HGRN_DK)
        q, k, log_f = _hgrn_gates(h_ref[:, sl(0)], h_ref[:, sl(1)], lb_all[:, h * HGRN_DK:(h + 1) * HGRN_DK])
        v = h_ref[:, sl(2)]
        s_new = _lane_to_sublane(jnp.exp(log_f)) * s0_ref[h] + _lane_to_sublane(k) * v
        s_ref[h] = s_new
        o = jnp.sum(_lane_to_sublane(q) * s_new, axis=0, keepdims=True)
        o_ref[:, h * HGRN_DV:(h + 1) * HGRN_DV] = _hgrn_out(o, h_ref[:, sl(3)], nw).astype(o_ref.dtype)


def _hgrn_decode(h4, s0, lbs_raw, norm_w, layer):
    n = h4.shape[0]
    state_spec = pl.BlockSpec((None, H_C, HGRN_DK, HGRN_DV), lambda b: (b, 0, 0, 0))
    o, s = pl.pallas_call(
        functools.partial(_hgrn_decode_kernel, layer=layer),
        grid=(n,),
        in_specs=[pl.BlockSpec((None, 1, 4 * D_MODEL), lambda b: (b, 0, 0)), state_spec,
                  pl.BlockSpec(lbs_raw.shape, lambda b: (0, 0)), pl.BlockSpec((1, HGRN_DV), lambda b: (0, 0))],
        out_specs=[pl.BlockSpec((None, 1, D_MODEL), lambda b: (b, 0, 0)), state_spec],
        out_shape=[jax.ShapeDtypeStruct((n, 1, D_MODEL), BF16), jax.ShapeDtypeStruct(s0.shape, F32)],
        compiler_params=_cparams("parallel"),
        name="hgrn_decode",
    )(h4.reshape(n, 1, 4 * D_MODEL), s0, lbs_raw, norm_w.reshape(1, HGRN_DV))
    return o.reshape(n, D_MODEL), s


MOE_TILE = 512


def _router_kernel(x_ref, w_ref, o_ref):
    logits = _dot(x_ref[...].astype(BF16), w_ref[...])
    lane = lax.broadcasted_iota(jnp.int32, logits.shape, 1)
    lg = jnp.where(lane < N_EXPERTS, logits, -jnp.inf)
    m1 = jnp.max(lg, axis=1, keepdims=True)
    i1 = jnp.min(jnp.where(lg == m1, lane, LANES), axis=1, keepdims=True)
    lg2 = jnp.where(lane == i1, -jnp.inf, lg)
    m2 = jnp.max(lg2, axis=1, keepdims=True)
    i2 = jnp.min(jnp.where(lg2 == m2, lane, LANES), axis=1, keepdims=True)
    e2 = jnp.exp(m2 - m1)
    g1 = 1.0 / (1.0 + e2)
    out = jnp.where(lane == 0, g1, jnp.where(lane == 1, e2 * g1, jnp.where(
        lane == 2, i1.astype(F32), jnp.where(lane == 3, i2.astype(F32), 0.0))))
    o_ref[...] = out


def _router(x, w_router_pad, tm=TOKEN_TILE):
    n = x.shape[0]
    return pl.pallas_call(
        _router_kernel,
        grid=(pl.cdiv(n, tm),),
        in_specs=[pl.BlockSpec((tm, D_MODEL), lambda i: (i, 0)), pl.BlockSpec((D_MODEL, LANES), lambda i: (0, 0))],
        out_specs=pl.BlockSpec((tm, LANES), lambda i: (i, 0)),
        out_shape=jax.ShapeDtypeStruct((n, LANES), F32),
        compiler_params=_cparams("parallel"),
        name="router",
    )(x, w_router_pad)


def _moe_in_kernel(te_ref, nu_ref, xs_ref, wa_ref, wb_ref, o_ref):
    used = pl.program_id(0) < nu_ref[0]

    @pl.when(used)
    def _():
        xb = xs_ref[...]
        a = _dot(xb, wa_ref[...])
        b = _dot(xb, wb_ref[...])
        o_ref[...] = (a * jax.nn.sigmoid(a) * b).astype(o_ref.dtype)

    @pl.when(jnp.logical_not(used))
    def _():
        o_ref[...] = jnp.zeros(o_ref.shape, o_ref.dtype)


def _moe_out_kernel(te_ref, nu_ref, h_ref, w_ref, o_ref):
    used = pl.program_id(0) < nu_ref[0]

    @pl.when(used)
    def _():
        o_ref[...] = _dot(h_ref[...], w_ref[...])

    @pl.when(jnp.logical_not(used))
    def _():
        o_ref[...] = jnp.zeros(o_ref.shape, o_ref.dtype)


def _moe_experts(xs, tile_expert, n_used, w_in, w_out, tn=512):
    p = xs.shape[0]
    f = w_out.shape[1]
    nj = f // tn
    nt = p // MOE_TILE
    h = pl.pallas_call(
        _moe_in_kernel,
        grid_spec=pltpu.PrefetchScalarGridSpec(
            num_scalar_prefetch=2, grid=(nt, nj),
            in_specs=[pl.BlockSpec((MOE_TILE, D_MODEL), lambda i, j, te, nu: (i, 0)),
                      pl.BlockSpec((None, D_MODEL, tn), lambda i, j, te, nu: (te[i], 0, j)),
                      pl.BlockSpec((None, D_MODEL, tn), lambda i, j, te, nu: (te[i], 0, j + nj))],
            out_specs=pl.BlockSpec((MOE_TILE, tn), lambda i, j, te, nu: (i, j))),
        out_shape=jax.ShapeDtypeStruct((p, f), BF16),
        compiler_params=_cparams("parallel", "arbitrary"),
        name="moe_in",
    )(tile_expert, n_used, xs, w_in, w_in)
    return pl.pallas_call(
        _moe_out_kernel,
        grid_spec=pltpu.PrefetchScalarGridSpec(
            num_scalar_prefetch=2, grid=(nt,),
            in_specs=[pl.BlockSpec((MOE_TILE, f), lambda i, te, nu: (i, 0)),
                      pl.BlockSpec((None, f, D_MODEL), lambda i, te, nu: (te[i], 0, 0))],
            out_specs=pl.BlockSpec((MOE_TILE, D_MODEL), lambda i, te, nu: (i, 0))),
        out_shape=jax.ShapeDtypeStruct((p, D_MODEL), F32),
        compiler_params=_cparams("parallel"),
        name="moe_out",
    )(tile_expert, n_used, h, w_out)


def _moe_plan(route):
    n = route.shape[0]
    e_idx = route[:, 2:4].astype(jnp.int32).reshape(-1)
    onehot = (e_idx[:, None] == jnp.arange(N_EXPERTS, dtype=jnp.int32)[None, :]).astype(jnp.int32)
    running = jnp.cumsum(onehot, axis=0)
    rank = jnp.take_along_axis(running, e_idx[:, None], axis=1)[:, 0] - 1
    counts = running[-1]
    tiles = (counts + MOE_TILE - 1) // MOE_TILE
    tile_end = jnp.cumsum(tiles)
    tile_start = tile_end - tiles
    nt = (2 * n + N_EXPERTS * (MOE_TILE - 1) + MOE_TILE - 1) // MOE_TILE
    dest = tile_start[e_idx] * MOE_TILE + rank
    token = jnp.zeros((nt * MOE_TILE,), jnp.int32).at[dest].set(jnp.arange(2 * n, dtype=jnp.int32) // 2)
    past_end = (jnp.arange(nt, dtype=jnp.int32)[:, None] >= tile_end[None, :]).astype(jnp.int32)
    tile_expert = jnp.minimum(jnp.sum(past_end, axis=1), N_EXPERTS - 1).astype(jnp.int32)
    return token, dest.reshape(n, 2), tile_expert, tile_end[-1:].astype(jnp.int32)


def _moe(x1, w_router_pad, w_in, w_out):
    route = _router(x1, w_router_pad)
    token, pos, tile_expert, n_used = _moe_plan(route)
    xs = x1.astype(BF16)[token]
    ys = _moe_experts(xs, tile_expert, n_used, w_in, w_out)
    return ys[pos[:, 0]], ys[pos[:, 1]], route


def kernel(x_prompt, x_sample, cache_attn, state_hgrn, page_table, p_prompt, p_sample, w_in_even, w_out_even,
           w_in_odd, w_out_odd, hgrn_lower_bounds, hgrn_norm_w, ln_mix_g, ln_mix_b, ln_ffn_g, ln_ffn_b,
           w_ffn_in, w_ffn_out, w_router, w_moe_in, w_moe_out, w_ple_proj, w_ple_gate):
    batch, t, d = x_prompt.shape
    n_seq = x_sample.shape[0]
    n = batch * t
    past_len = page_table.shape[1] * PAGE_SIZE
    x = jnp.concatenate([x_prompt.reshape(n, d), x_sample.reshape(n_seq, d)], axis=0)
    pos = jnp.concatenate([jnp.tile(jnp.arange(t, dtype=jnp.int32), batch), jnp.full((n_seq,), past_len, jnp.int32)])
    cs = _rotary_table(pos)
    p_all = jnp.concatenate([p_prompt.reshape(DEPTH, n, PLE_DIM), p_sample.reshape(DEPTH, n_seq, PLE_DIM)], axis=1)
    join = lambda a, b: jnp.concatenate([a, b], axis=0)
    cache_t = jnp.swapaxes(cache_attn, 2, 3)

    rows_p, rows_s, state_p, state_s = [], [], [], []
    for i in range(DEPTH):
        j = i // 2
        ple = (p_all[i], w_ple_gate[i].astype(BF16), w_ple_proj[i].astype(BF16))
        if i % 2 == 0:
            w_rows, w_q = _prep_even_in(w_in_even[j])
            rows, qa, qi, wi, qb, ka, va, ki, kb, vb = _even_inproj(x, w_rows, w_q, cs)
            oa_p = _dsa_prompt(qa, qi, wi, ka, va, ki, batch, t)
            ob_p = _sb_prompt(qb, kb, vb, batch, t)
            oa_s, ob_s = _even_decode(page_table, cache_t, j, qa[n:], qi[n:], wi[n:], qb[n:], rows[n:])
            w_o = w_out_even[j].astype(BF16)
            x1 = _mix_ln(x, ln_mix_g[i], ln_mix_b[i],
                         mm=[(join(oa_p, oa_s), w_o[:WIDTH_A]), (join(ob_p, ob_s), w_o[WIDTH_A:])])
            h = _swiglu_in(x1, w_ffn_in[j].astype(BF16))
            x = _mix_ln(x1, ln_ffn_g[i], ln_ffn_b[i], mm=[(h, w_ffn_out[j].astype(BF16))], ple=ple)
            rows_p.append(rows[:n].reshape(batch, t, ROW_DIM))
            rows_s.append(rows[n:].reshape(n_seq, 1, ROW_DIM))
        else:
            h4 = _matmul(x, w_in_odd[j].astype(BF16))
            o_p, s_p = _hgrn_prompt(h4, hgrn_lower_bounds, hgrn_norm_w[j], batch, t, j)
            o_s, s_s = _hgrn_decode(h4[n:], state_hgrn[j], hgrn_lower_bounds, hgrn_norm_w[j], j)
            x1 = _mix_ln(x, ln_mix_g[i], ln_mix_b[i], mm=[(join(o_p, o_s), w_out_odd[j].astype(BF16))])
            w_r = jnp.pad(w_router[j], ((0, 0), (0, LANES - N_EXPERTS))).astype(BF16)
            y1, y2, route = _moe(x1, w_r, w_moe_in[j].astype(BF16), w_moe_out[j].astype(BF16))
            x = _mix_ln(x1, ln_ffn_g[i], ln_ffn_b[i], comb=(y1, y2), gates=route, ple=ple)
            state_p.append(s_p)
            state_s.append(s_s)
    return (x[:n].reshape(batch, t, d), x[n:].reshape(n_seq, 1, d), jnp.stack(rows_p), jnp.stack(rows_s),
            jnp.stack(state_p), jnp.stack(state_s))
```

```python
import functools

import jax
import jax.numpy as jnp
import numpy as np
from jax import lax
from jax.experimental import pallas as pl
from jax.experimental.pallas import tpu as pltpu

D_MODEL = 1024
DEPTH = 4
PAGE_SIZE = 128
HEAD_DIM = 64
N_HEADS_A = 8
N_HEADS_B = 8
IDX_HEADS = 4
IDX_DIM = 64
TOPK_MAX = 256
ROPE_THETA = 500000.0
H_C = 8
HGRN_DK = 128
HGRN_DV = D_MODEL // H_C
D_FF = 2816
N_EXPERTS = 8
D_FF_EXPERT = 3584
PLE_DIM = 256
DEEPNORM_ALPHA = (2 * DEPTH) ** 0.25
LN_EPS = 1e-5
RMS_EPS = 1e-6
NEG_BIG = -1e30
WIDTH_A = N_HEADS_A * HEAD_DIM
WIDTH_B = N_HEADS_B * HEAD_DIM
ROW_DIM = 2 * WIDTH_A + IDX_DIM + 2 * WIDTH_B

LANES = 128
HALF = LANES // 2
VMEM_LIMIT = 56 * 1024 * 1024
TOKEN_TILE = 384
INT_MIN = -2 ** 31
NO_INDEX = 2 ** 30
EXP_IS_ZERO = -104.0

BF16 = jnp.bfloat16
F32 = jnp.float32


def _cparams(*sem, vmem=VMEM_LIMIT):
    return pltpu.CompilerParams(dimension_semantics=sem, vmem_limit_bytes=vmem)


def _dot(a, b):
    return jnp.dot(a, b, preferred_element_type=F32)


def _dot_nt(a, b):
    return lax.dot_general(a, b, (((1,), (1,)), ((), ())), preferred_element_type=F32)


def _dot_tn(a, b):
    return lax.dot_general(a, b, (((0,), (0,)), ((), ())), preferred_element_type=F32)


def _low_half(shape):
    return lax.broadcasted_iota(jnp.int32, shape, len(shape) - 1) < HALF


def _split_hi_lo(x):
    hi = x.astype(BF16)
    lo = (x - hi.astype(F32)).astype(BF16)
    return hi, lo


def _rotary(t, c, s1, s2):
    return t * c + pltpu.roll(t, LANES - 8, 1) * s1 + pltpu.roll(t, 8, 1) * s2


def _even_inproj_kernel(x_ref, wr_ref, wq_ref, cs_ref, rows_ref, qa_ref, qi_ref, wi_ref, qb_ref,
                        ka_ref, va_ref, ki_ref, kb_ref, vb_ref):
    xb = x_ref[...].astype(BF16)
    c = cs_ref[:, 0:LANES]
    s1 = cs_ref[:, LANES:2 * LANES]
    s2 = cs_ref[:, 2 * LANES:3 * LANES]
    low = _low_half(c.shape)
    acc = _dot(xb, wr_ref[...])
    for t in range(WIDTH_A // LANES):
        sl = slice(t * LANES, (t + 1) * LANES)
        r = _rotary(acc[:, sl], c, s1, s2)
        rows_ref[:, sl] = r
        ka_ref[:, sl] = r.astype(BF16)
    v_a = acc[:, WIDTH_A:2 * WIDTH_A]
    rows_ref[:, WIDTH_A:2 * WIDTH_A] = v_a
    va_ref[...] = v_a.astype(BF16)
    o_i = 2 * WIDTH_A
    t8 = acc[:, o_i:o_i + LANES]
    r8 = _rotary(t8, jnp.where(low, c, 1.0), jnp.where(low, s1, 0.0), jnp.where(low, s2, 0.0))
    rows_ref[:, o_i:o_i + LANES] = r8
    ki_ref[...] = jnp.where(low, r8, pltpu.roll(r8, HALF, 1)).astype(BF16)
    rows_ref[:, o_i + LANES:ROW_DIM] = acc[:, o_i + LANES:ROW_DIM]
    o_b = o_i + IDX_DIM
    kb_ref[...] = acc[:, o_b:o_b + WIDTH_B].astype(BF16)
    vb_ref[...] = acc[:, o_b + WIDTH_B:o_b + 2 * WIDTH_B].astype(BF16)

    accq = _dot(xb, wq_ref[...])
    scale = HEAD_DIM ** -0.5

    def put_pairs(dst_ref, src_off, n_pairs, rot):
        for p in range(n_pairs):
            t = accq[:, src_off + p * LANES:src_off + (p + 1) * LANES]
            if rot:
                t = _rotary(t, c, s1, s2)
            t = t * scale
            dst_ref[:, (2 * p) * LANES:(2 * p + 1) * LANES] = jnp.where(low, t, 0.0).astype(BF16)
            dst_ref[:, (2 * p + 1) * LANES:(2 * p + 2) * LANES] = jnp.where(low, 0.0, t).astype(BF16)

    put_pairs(qa_ref, 0, N_HEADS_A // 2, True)
    put_pairs(qi_ref, WIDTH_A, IDX_HEADS // 2, True)
    put_pairs(qb_ref, WIDTH_A + IDX_HEADS * IDX_DIM, N_HEADS_B // 2, False)
    o_w = WIDTH_A + IDX_HEADS * IDX_DIM + WIDTH_B
    wi_ref[...] = accq[:, o_w:o_w + LANES] * IDX_HEADS ** -0.5


def _even_inproj(x, w_rows, w_q, cs, tm=TOKEN_TILE):
    n = x.shape[0]
    qw = w_q.shape[1]
    row = lambda i: (i, 0)
    const = lambda i: (0, 0)
    outs = [
        (ROW_DIM, F32), (2 * WIDTH_A, BF16), (2 * IDX_HEADS * IDX_DIM, BF16), (LANES, F32), (2 * WIDTH_B, BF16),
        (WIDTH_A, BF16), (WIDTH_A, BF16), (LANES, BF16), (WIDTH_B, BF16), (WIDTH_B, BF16),
    ]
    return pl.pallas_call(
        _even_inproj_kernel,
        grid=(pl.cdiv(n, tm),),
        in_specs=[pl.BlockSpec((tm, D_MODEL), row), pl.BlockSpec((D_MODEL, ROW_DIM), const),
                  pl.BlockSpec((D_MODEL, qw), const), pl.BlockSpec((tm, 3 * LANES), row)],
        out_specs=[pl.BlockSpec((tm, w), row) for w, _ in outs],
        out_shape=[jax.ShapeDtypeStruct((n, w), d) for w, d in outs],
        compiler_params=_cparams("parallel"),
        name="even_inproj",
    )(x, w_rows, w_q, cs)


def _matmul_kernel(x_ref, w_ref, o_ref):
    o_ref[...] = _dot(x_ref[...].astype(BF16), w_ref[...]).astype(o_ref.dtype)


def _matmul(x, w, out_dtype=F32, tm=TOKEN_TILE, tn=1024):
    n, k = x.shape
    m = w.shape[1]
    tn = min(tn, m)
    return pl.pallas_call(
        _matmul_kernel,
        grid=(pl.cdiv(n, tm), m // tn),
        in_specs=[pl.BlockSpec((tm, k), lambda i, j: (i, 0)), pl.BlockSpec((k, tn), lambda i, j: (0, j))],
        out_specs=pl.BlockSpec((tm, tn), lambda i, j: (i, j)),
        out_shape=jax.ShapeDtypeStruct((n, m), out_dtype),
        compiler_params=_cparams("parallel", "arbitrary"),
        name="matmul",
    )(x, w)


def _swiglu_in_kernel(x_ref, wa_ref, wb_ref, o_ref):
    xb = x_ref[...].astype(BF16)
    a = _dot(xb, wa_ref[...])
    b = _dot(xb, wb_ref[...])
    o_ref[...] = (a * jax.nn.sigmoid(a) * b).astype(o_ref.dtype)


def _swiglu_in(x, w_in, tm=TOKEN_TILE, tn=1408):
    n, k = x.shape
    f = w_in.shape[1] // 2
    nj = f // tn
    return pl.pallas_call(
        _swiglu_in_kernel,
        grid=(pl.cdiv(n, tm), nj),
        in_specs=[pl.BlockSpec((tm, k), lambda i, j: (i, 0)),
                  pl.BlockSpec((k, tn), lambda i, j: (0, j)),
                  pl.BlockSpec((k, tn), lambda i, j: (0, j + nj))],
        out_specs=pl.BlockSpec((tm, tn), lambda i, j: (i, j)),
        out_shape=jax.ShapeDtypeStruct((n, f), BF16),
        compiler_params=_cparams("parallel", "arbitrary"),
        name="swiglu_in",
    )(x, w_in, w_in)


def _layernorm(y, g, b):
    mu = jnp.mean(y, axis=-1, keepdims=True)
    d = y - mu
    var = jnp.mean(d * d, axis=-1, keepdims=True)
    return d * lax.rsqrt(var + LN_EPS) * g + b


def _mix_ln_kernel(n_mm, n_comb, ple, *refs):
    it = iter(refs)
    mm = [(next(it), next(it)) for _ in range(n_mm)]
    comb = [next(it) for _ in range(n_comb)]
    gates_ref = next(it) if n_comb else None
    x_ref, g_ref, b_ref = next(it), next(it), next(it)
    if ple:
        p_ref, wg_ref, wp_ref = next(it), next(it), next(it)
    o_ref = next(it)
    y = DEEPNORM_ALPHA * x_ref[...]
    for h_ref, w_ref in mm:
        y = y + _dot(h_ref[...], w_ref[...])
    for k, y_ref in enumerate(comb):
        y = y + gates_ref[:, k:k + 1] * y_ref[...]
    y = _layernorm(y, g_ref[...], b_ref[...])
    if ple:
        gate = jax.nn.sigmoid(_dot(y.astype(BF16), wg_ref[...]))
        y = y + gate * _dot(p_ref[...].astype(BF16), wp_ref[...])
    o_ref[...] = y


def _mix_ln(x, g, b, mm=(), comb=(), gates=None, ple=None, tm=TOKEN_TILE):
    n = x.shape[0]
    row = lambda i: (i, 0)
    const = lambda i: (0, 0)
    args, specs = [], []
    for h, w in mm:
        args += [h, w]
        specs += [pl.BlockSpec((tm, h.shape[1]), row), pl.BlockSpec(w.shape, const)]
    for y in comb:
        args.append(y)
        specs.append(pl.BlockSpec((tm, D_MODEL), row))
    if comb:
        args.append(gates)
        specs.append(pl.BlockSpec((tm, gates.shape[1]), row))
    args += [x, g.reshape(1, D_MODEL), b.reshape(1, D_MODEL)]
    specs += [pl.BlockSpec((tm, D_MODEL), row), pl.BlockSpec((1, D_MODEL), const), pl.BlockSpec((1, D_MODEL), const)]
    if ple is not None:
        p, wg, wp = ple
        args += [p, wg, wp]
        specs += [pl.BlockSpec((tm, PLE_DIM), row), pl.BlockSpec(wg.shape, const), pl.BlockSpec(wp.shape, const)]
    return pl.pallas_call(
        functools.partial(_mix_ln_kernel, len(mm), len(comb), ple is not None),
        grid=(pl.cdiv(n, tm),),
        in_specs=specs,
        out_specs=pl.BlockSpec((tm, D_MODEL), row),
        out_shape=jax.ShapeDtypeStruct((n, D_MODEL), F32),
        compiler_params=_cparams("parallel"),
        name="mix_ln",
    )(*args)


def _prep_even_in(w):
    sizes = (WIDTH_A, WIDTH_A, WIDTH_A, IDX_HEADS * IDX_DIM, IDX_DIM, IDX_HEADS, WIDTH_B, WIDTH_B, WIDTH_B)
    qa, ka, va, qi, ki, wi, qb, kb, vb = jnp.split(w, np.cumsum(sizes)[:-1].tolist(), axis=1)
    w_rows = jnp.concatenate([ka, va, ki, kb, vb], axis=1)
    pad = jnp.zeros((w.shape[0], LANES - IDX_HEADS), w.dtype)
    w_q = jnp.concatenate([qa, qi, qb, wi, pad], axis=1)
    return w_rows.astype(BF16), w_q.astype(BF16)


def _rotary_table(pos):
    rd = HEAD_DIM // 4
    half = rd // 2
    inv_freq = ROPE_THETA ** (-jnp.arange(half, dtype=F32) * 2.0 / rd)
    ang = pos.astype(F32)[:, None] * inv_freq[None, :]
    cos, sin = jnp.cos(ang), jnp.sin(ang)
    n = pos.shape[0]
    ones = jnp.ones((n, HEAD_DIM - rd), F32)
    zeros = jnp.zeros((n, HEAD_DIM - rd), F32)
    zh = jnp.zeros((n, half), F32)
    c = jnp.concatenate([cos, cos, ones], axis=1)
    s1 = jnp.concatenate([-sin, zh, zeros], axis=1)
    s2 = jnp.concatenate([zh, sin, zeros], axis=1)
    return jnp.concatenate([c, c, s1, s1, s2, s2], axis=1)


def _softplus(z):
    return jnp.maximum(z, 0.0) + jnp.log(1.0 + jnp.exp(-jnp.abs(z)))


def _sb_tiles(qs, kvs, upper2, state, valid):
    work = [(c, kt, vt) for kt, vt in kvs for c in range(len(qs))]
    zs = [_dot_nt(qs[c], kt) for c, kt, _ in work]
    sps = [_softplus(z) for z in zs]
    sp_ms = sps if valid is None else [jnp.where(valid, sp, 0.0) for sp in sps]
    sums = [_dot(jnp.concatenate(_split_hi_lo(sp_m), axis=1), upper2) for sp_m in sp_ms]
    carries = [carry for carry, _ in state]
    ws = []
    for (c, _, _), z, sp, sp_m, x in zip(work, zs, sps, sp_ms, sums):
        w = jnp.exp(z - sp + (carries[c] - x))
        ws.append((w if valid is None else jnp.where(valid, w, 0.0)).astype(BF16))
        carries[c] = carries[c] - jnp.sum(sp_m, axis=1, keepdims=True)
    accs = [acc for _, acc in state]
    for (c, _, vt), w in zip(work, ws):
        accs[c] = accs[c] + _dot(w, vt)
    return tuple(zip(carries, accs))


def _sb_prompt_kernel(q_ref, k_ref, v_ref, o_ref, *, tq, unroll):
    qi = pl.program_id(2)
    rows = lax.broadcasted_iota(jnp.int32, (tq, tq), 0)
    cols = lax.broadcasted_iota(jnp.int32, (tq, tq), 1)
    upper = (rows > cols).astype(BF16)
    upper2 = jnp.concatenate([upper, upper], axis=0)
    qs = [q_ref[:, 0:LANES], q_ref[:, LANES:2 * LANES]]

    def tiles(j_first, n, state, valid):
        kvs = []
        for u in range(n):
            start = pl.multiple_of((j_first - u) * tq, tq)
            kvs.append((k_ref[pl.ds(start, tq), :], v_ref[pl.ds(start, tq), :]))
        return _sb_tiles(qs, kvs, upper2, state, valid)

    zero = (jnp.zeros((tq, 1), F32), jnp.zeros((tq, LANES), F32))
    state = tiles(qi, 1, (zero, zero), cols < rows)

    def live(st):
        return jnp.max(jnp.maximum(st[0][0], st[1][0])) > EXP_IS_ZERO

    n_left = qi % unroll
    for u in range(1, unroll):
        state = lax.cond(n_left >= u, lambda st, u=u: tiles(qi - u, 1, st, None), lambda st: st, state)
    first = qi - 1 - n_left

    def step(c):
        jj, st, _ = c
        st = tiles(first - unroll * jj, unroll, st, None)
        return jj + 1, st, live(st)

    _, state, _ = lax.while_loop(lambda c: jnp.logical_and(c[0] < qi // unroll, c[2]), step,
                                 (jnp.int32(0), state, live(state)))
    low = _low_half((tq, LANES))
    o_ref[...] = jnp.where(low, state[0][1], state[1][1]).astype(o_ref.dtype)


def _sb_prompt(q_pad, k, v, batch, t, tq=256, unroll=1):
    tq = min(tq, t)
    nq = t // tq
    return pl.pallas_call(
        functools.partial(_sb_prompt_kernel, tq=tq, unroll=unroll),
        grid=(batch, N_HEADS_B // 2, nq),
        in_specs=[pl.BlockSpec((tq, 2 * LANES), lambda b, p, i: (b * nq + i, p)),
                  pl.BlockSpec((t, LANES), lambda b, p, i: (b, p)),
                  pl.BlockSpec((t, LANES), lambda b, p, i: (b, p))],
        out_specs=pl.BlockSpec((tq, LANES), lambda b, p, i: (b * nq + i, p)),
        out_shape=jax.ShapeDtypeStruct((batch * t, WIDTH_B), BF16),
        compiler_params=_cparams("parallel", "parallel", "arbitrary"),
        name="sb_prompt",
    )(q_pad, k, v)


def _sortable(x):
    b = lax.bitcast_convert_type(x + 0.0, jnp.int32)
    return b ^ ((b >> 31) & 0x7FFFFFFF)


def _dsa_prompt_kernel(qa_ref, qi_ref, wi_ref, ka_ref, va_ref, ki_ref, o_ref, key_sc, bias_sc, mx_sc, acc_sc,
                       *, tq, tk, n_sel, t):
    q0 = pl.program_id(1) * tq
    nkt = (q0 + tq + tk - 1) // tk
    rows = q0 + lax.broadcasted_iota(jnp.int32, (tq, tk), 0)
    cols0 = lax.broadcasted_iota(jnp.int32, (tq, tk), 1)
    w = wi_ref[...]

    def score_tile(j, carry):
        start = pl.multiple_of(j * tk, tk)
        kt = ki_ref[pl.ds(start, tk), :]
        ss = [_dot_nt(qi_ref[:, h * LANES:(h + 1) * LANES], kt) for h in range(IDX_HEADS)]
        score = jnp.zeros((tq, tk), F32)
        for h in range(IDX_HEADS):
            score = score + jnp.maximum(ss[h], 0.0) * w[:, h:h + 1]
        score = jnp.where(start + cols0 <= rows, score, NEG_BIG)
        key_sc[:, pl.ds(start, tk)] = _sortable(score)
        return carry

    lax.fori_loop(0, nkt, score_tile, 0)

    def count(pred):
        def body(j, c):
            start = pl.multiple_of(j * tk, tk)
            hit = jnp.where(pred(key_sc[:, pl.ds(start, tk)], start + cols0), 1.0, 0.0)
            for u in range(tk // LANES):
                c = c + hit[:, u * LANES:(u + 1) * LANES]
            return c
        c = lax.fori_loop(0, nkt, body, jnp.zeros((tq, LANES), F32))
        return jnp.sum(c, axis=1, keepdims=True)

    def bit_step(i, v):
        cand = v ^ lax.shift_left(jnp.int32(1), 31 - i)
        return jnp.where(count(lambda kk, kp: kk >= cand) >= n_sel, cand, v)

    v = lax.fori_loop(0, 32, bit_step, jnp.full((tq, 1), INT_MIN, jnp.int32))
    r = n_sel - count(lambda kk, kp: kk > v)
    n_eq = count(lambda kk, kp: kk == v)

    def tie_cut():
        nbits = (t - 1).bit_length()

        def step(i, c):
            cand = c + lax.shift_left(jnp.int32(1), nbits - 1 - i)
            g = count(lambda kk, kp: jnp.where(kk == v, kp, NO_INDEX) < cand)
            return jnp.where(g < r, cand, c)

        return lax.fori_loop(0, nbits, step, jnp.zeros((tq, 1), jnp.int32))

    cut = lax.cond(jnp.max(n_eq - r) > 0, tie_cut, lambda: jnp.full((tq, 1), t, jnp.int32))

    heads = range(N_HEADS_A)

    def logits(start):
        kts = [ka_ref[pl.ds(start, tk), p * LANES:(p + 1) * LANES] for p in range(N_HEADS_A // 2)]
        return [_dot_nt(qa_ref[:, h * LANES:(h + 1) * LANES], kts[h // 2]) for h in heads]

    mx_sc[...] = jnp.full(mx_sc.shape, NEG_BIG, F32)

    def max_tile(j, carry):
        start = pl.multiple_of(j * tk, tk)
        kk = key_sc[:, pl.ds(start, tk)]
        kp = start + cols0
        tie_ok = jnp.where(kk == v, kp, NO_INDEX) <= cut
        bias = jnp.where(kp <= rows, jnp.where(kk > v, 0.0, jnp.where(tie_ok, 0.0, NEG_BIG)), NEG_BIG)
        bias_sc[:, pl.ds(start, tk)] = bias
        for h, s in zip(heads, logits(start)):
            mx_sc[h] = jnp.maximum(mx_sc[h], s + bias)
        return carry

    lax.fori_loop(0, nkt, max_tile, 0)
    m = [jnp.max(mx_sc[h], axis=1, keepdims=True) for h in heads]
    mx_sc[...] = jnp.zeros(mx_sc.shape, F32)
    acc_sc[...] = jnp.zeros(acc_sc.shape, F32)

    def sum_tile(j, carry):
        start = pl.multiple_of(j * tk, tk)
        bias = bias_sc[:, pl.ds(start, tk)]
        vts = [va_ref[pl.ds(start, tk), p * LANES:(p + 1) * LANES] for p in range(N_HEADS_A // 2)]
        prs = []
        for h, s in zip(heads, logits(start)):
            pr = jnp.exp(s + bias - m[h])
            mx_sc[h] = mx_sc[h] + pr
            prs.append(pr.astype(BF16))
        for h in heads:
            acc_sc[h] = acc_sc[h] + _dot(prs[h], vts[h // 2])
        return carry

    lax.fori_loop(0, nkt, sum_tile, 0)
    low = _low_half((tq, LANES))
    outs = [acc_sc[h] / jnp.sum(mx_sc[h], axis=1, keepdims=True) for h in heads]
    for p in range(N_HEADS_A // 2):
        o_ref[:, p * LANES:(p + 1) * LANES] = jnp.where(low, outs[2 * p], outs[2 * p + 1]).astype(o_ref.dtype)


def _dsa_prompt(qa_pad, qi_pad, wi, ka, va, ki2, batch, t, tq=128, tk=512):
    tq = min(tq, t)
    tk = min(tk, t)
    nq = t // tq
    n_sel = max(1, min(TOPK_MAX, t // 4))
    qrow = lambda b, i: (b * nq + i, 0)
    whole_seq = lambda w: pl.BlockSpec((t, w), lambda b, i: (b, 0), pipeline_mode=pl.Buffered(1))
    return pl.pallas_call(
        functools.partial(_dsa_prompt_kernel, tq=tq, tk=tk, n_sel=n_sel, t=t),
        grid=(batch, nq),
        in_specs=[pl.BlockSpec((tq, 2 * WIDTH_A), qrow), pl.BlockSpec((tq, 2 * IDX_HEADS * IDX_DIM), qrow),
                  pl.BlockSpec((tq, LANES), qrow), whole_seq(WIDTH_A), whole_seq(WIDTH_A), whole_seq(LANES)],
        out_specs=pl.BlockSpec((tq, WIDTH_A), qrow),
        out_shape=jax.ShapeDtypeStruct((batch * t, WIDTH_A), BF16),
        scratch_shapes=[pltpu.VMEM((tq, t), jnp.int32), pltpu.VMEM((tq, t), F32),
                        pltpu.VMEM((N_HEADS_A, tq, tk), F32), pltpu.VMEM((N_HEADS_A, tq, LANES), F32)],
        compiler_params=_cparams("parallel", "arbitrary"),
        name="dsa_prompt",
    )(qa_pad, qi_pad, wi, ka, va, ki2)


OFF_KI = 2 * WIDTH_A
OFF_KB = OFF_KI + IDX_DIM
OFF_VB = OFF_KB + WIDTH_B


def _head_rows(row, off, n_rows=8):
    w = row.shape[1]
    col = lax.broadcasted_iota(jnp.int32, (n_rows, w), 1)
    hrow = lax.broadcasted_iota(jnp.int32, (n_rows, w), 0)
    head_of_col = ((col - off + HEAD_DIM) >> 6) - 1
    return jnp.where(head_of_col == hrow, jnp.broadcast_to(row, (n_rows, w)), 0.0)


def _sum_all(x):
    return jnp.sum(jnp.sum(x, axis=0, keepdims=True), axis=1, keepdims=True)


def _even_decode_kernel(pt_ref, qa_ref, qi_ref, wi_ref, qb_ref, self_ref, *rest, n_pages, n_sel):
    pages = rest[:n_pages]
    oa_ref, ob_ref, key_sc, la_sc, z_sc = rest[n_pages:]
    n_past = n_pages * PAGE_SIZE
    f32_tile = lambda ref, h: ref[:, h * LANES:(h + 1) * LANES].astype(F32)
    swap = lambda t: pltpu.roll(t, HALF, 1)

    compact = lambda ref, n_heads: jnp.concatenate(
        [f32_tile(ref, 2 * p) + f32_tile(ref, 2 * p + 1) for p in range(n_heads // 2)], axis=1)
    qa_bd16 = _head_rows(compact(qa_ref, N_HEADS_A), 0).astype(BF16)
    qb_bd16 = _head_rows(compact(qb_ref, N_HEADS_B), 0).astype(BF16)
    head_mask = _head_rows(jnp.ones((1, WIDTH_A), F32), 0)
    page16 = lambda s, off, width: pages[s][off:off + width, :].astype(BF16)
    hrow = lax.broadcasted_iota(jnp.int32, (8, LANES), 0)
    lane8 = lax.broadcasted_iota(jnp.int32, (8, LANES), 1)
    qi_rows = jnp.zeros((8, LANES), F32)
    for h in range(IDX_HEADS):
        t = f32_tile(qi_ref, h)
        t = swap(t) if h % 2 else t
        qi_rows = jnp.where(hrow == h, jnp.broadcast_to(t, (8, LANES)), qi_rows)
    qi_rows16 = qi_rows.astype(BF16)
    w_col = jnp.sum(jnp.where(lane8 == hrow, jnp.broadcast_to(wi_ref[...], (8, LANES)), 0.0), axis=1, keepdims=True)

    def idx_score(s_i):
        return jnp.sum(jnp.maximum(s_i, 0.0) * w_col, axis=0, keepdims=True)

    for s in range(n_pages):
        key_sc[s:s + 1, :] = _sortable(idx_score(_dot(qi_rows16, page16(s, OFF_KI, LANES))))
        la_sc[s] = _dot(qa_bd16, page16(s, 0, WIDTH_A))
        z_sc[s] = _dot(qb_bd16, page16(s, OFF_KB, WIDTH_B))

    rnd = lambda x: x.astype(BF16).astype(F32)
    s_self = jnp.sum(qi_rows16.astype(F32) * rnd(self_ref[:, OFF_KI:OFF_KI + LANES]), axis=1, keepdims=True)
    sc_self = idx_score(s_self)
    la_self = jnp.sum(qa_bd16.astype(F32) * rnd(self_ref[:, 0:WIDTH_A]), axis=1, keepdims=True)
    lane1 = lax.broadcasted_iota(jnp.int32, (1, LANES), 1)
    key_sc[n_pages:n_pages + 1, :] = jnp.where(lane1 == 0, _sortable(jnp.broadcast_to(sc_self, (1, LANES))), INT_MIN)

    keys = key_sc[...]
    kp = (lax.broadcasted_iota(jnp.int32, keys.shape, 0) * PAGE_SIZE
          + lax.broadcasted_iota(jnp.int32, keys.shape, 1))
    count = lambda pred: _sum_all(jnp.where(pred, 1, 0))

    def bit_step(i, v):
        cand = v ^ lax.shift_left(jnp.int32(1), 31 - i)
        return jnp.where(count(keys >= cand) >= n_sel, cand, v)

    v = lax.fori_loop(0, 32, bit_step, jnp.full((1, 1), INT_MIN, jnp.int32))
    r = n_sel - count(keys > v)
    n_eq = count(keys == v)
    tie_pos = jnp.where(keys == v, kp, NO_INDEX)

    def tie_cut():
        nbits = n_past.bit_length()

        def step(i, c):
            cand = c + lax.shift_left(jnp.int32(1), nbits - 1 - i)
            return jnp.where(count(tie_pos < cand) < r, cand, c)

        return lax.fori_loop(0, nbits, step, jnp.zeros((1, 1), jnp.int32))

    cut = lax.cond(jnp.max(n_eq - r) > 0, tie_cut, lambda: jnp.full((1, 1), NO_INDEX - 1, jnp.int32))
    bias = jnp.where(kp <= n_past, jnp.where(keys > v, 0.0, jnp.where(tie_pos <= cut, 0.0, NEG_BIG)), NEG_BIG)

    bias_self = bias[n_pages:n_pages + 1, 0:1]
    m = la_self + bias_self
    for s in range(n_pages):
        m = jnp.maximum(m, jnp.max(la_sc[s] + bias[s:s + 1, :], axis=1, keepdims=True))
    p_self = jnp.exp(la_self + bias_self - m)
    l = p_self
    acc = rnd(p_self) * rnd(self_ref[:, WIDTH_A:2 * WIDTH_A])
    for s in range(n_pages):
        pr = jnp.exp(la_sc[s] + bias[s:s + 1, :] - m)
        l = l + jnp.sum(pr, axis=1, keepdims=True)
        acc = acc + _dot_nt(pr.astype(BF16), page16(s, WIDTH_A, WIDTH_A))
    oa_ref[...] = jnp.sum(head_mask * (acc / l), axis=0, keepdims=True).astype(oa_ref.dtype)

    t_idx = lax.broadcasted_iota(jnp.int32, (PAGE_SIZE, PAGE_SIZE), 0)
    s_idx = lax.broadcasted_iota(jnp.int32, (PAGE_SIZE, PAGE_SIZE), 1)
    upper = (t_idx > s_idx).astype(BF16)
    carry = jnp.zeros((8, 1), F32)
    acc_b = jnp.zeros((8, WIDTH_B), F32)
    for s in reversed(range(n_pages)):
        z = z_sc[s]
        sp = _softplus(z)
        hi, lo = _split_hi_lo(-sp)
        later = _dot(hi, upper) + _dot(lo, upper) + carry
        w = jnp.exp(z - sp + later)
        acc_b = acc_b + _dot_nt(w.astype(BF16), page16(s, OFF_VB, WIDTH_B))
        carry = carry - jnp.sum(sp, axis=1, keepdims=True)
    ob_ref[...] = jnp.sum(head_mask * acc_b, axis=0, keepdims=True).astype(ob_ref.dtype)


def _even_decode(page_table, cache_t, layer, qa_pad, qi_pad, wi, qb_pad, self_rows):
    n, n_pages = page_table.shape
    n_sel = max(1, min(TOPK_MAX, (n_pages * PAGE_SIZE + 1) // 4))
    per_seq = lambda w: pl.BlockSpec((None, 1, w), lambda b, pt: (b, 0, 0))
    page_spec = lambda s: pl.BlockSpec((None, None, ROW_DIM, PAGE_SIZE), lambda b, pt: (layer, pt[b, s], 0, 0))
    r3 = lambda a: a.reshape(n, 1, a.shape[-1])
    oa, ob = pl.pallas_call(
        functools.partial(_even_decode_kernel, n_pages=n_pages, n_sel=n_sel),
        grid_spec=pltpu.PrefetchScalarGridSpec(
            num_scalar_prefetch=1, grid=(n,),
            in_specs=[per_seq(2 * WIDTH_A), per_seq(2 * IDX_HEADS * IDX_DIM), per_seq(LANES), per_seq(2 * WIDTH_B),
                      per_seq(ROW_DIM)] + [page_spec(s) for s in range(n_pages)],
            out_specs=[per_seq(WIDTH_A), per_seq(WIDTH_B)],
            scratch_shapes=[pltpu.VMEM((n_pages + 1, LANES), jnp.int32), pltpu.VMEM((n_pages, 8, LANES), F32),
                            pltpu.VMEM((n_pages, 8, LANES), F32)]),
        out_shape=[jax.ShapeDtypeStruct((n, 1, WIDTH_A), BF16), jax.ShapeDtypeStruct((n, 1, WIDTH_B), BF16)],
        compiler_params=_cparams("parallel"),
        name="even_decode",
    )(page_table, r3(qa_pad), r3(qi_pad), r3(wi), r3(qb_pad), r3(self_rows), *([cache_t] * n_pages))
    return oa.reshape(n, WIDTH_A), ob.reshape(n, WIDTH_B)


def _hgrn_lower_bound(lb_ref, layer):
    x = lb_ref[...]
    e = jnp.exp(x - jnp.max(x, axis=0, keepdims=True))
    sm = e / jnp.sum(e, axis=0, keepdims=True)
    lb = jnp.zeros((1, sm.shape[1]), F32)
    for r in range(1, layer + 1):
        lb = lb + sm[r:r + 1, :]
    return lb


def _hgrn_gates(qr, fl, lb):
    q = qr * jax.nn.sigmoid(qr) * HGRN_DK ** -0.5
    log_f = jnp.log(lb + (1.0 - lb) * jax.nn.sigmoid(fl))
    k = (1.0 - lb) * jax.nn.sigmoid(-fl)
    return q, k, log_f


def _hgrn_out(o, gr, nw):
    o = o * lax.rsqrt(jnp.mean(o * o, axis=-1, keepdims=True) + RMS_EPS) * nw
    return o * (gr * jax.nn.sigmoid(gr))


def _hgrn_prompt_kernel(q_ref, f_ref, i_ref, g_ref, lb_ref, nw_ref, o_ref, s_ref, st_sc, *, c, m, hp, layer):
    ci = pl.program_id(2)

    @pl.when(ci == 0)
    def _():
        st_sc[...] = jnp.zeros(st_sc.shape, F32)

    heads = range(hp)
    hs = lambda ref, h: ref[:, h * HGRN_DK:(h + 1) * HGRN_DK]
    lb_all = _hgrn_lower_bound(lb_ref, layer)
    gates = [_hgrn_gates(hs(q_ref, h), hs(f_ref, h), hs(lb_all, h)) for h in heads]
    q = [g[0] for g in gates]
    k = [g[1] for g in gates]
    v = [hs(i_ref, h) for h in heads]
    vb = [x.astype(BF16) for x in v]
    t_idx = lax.broadcasted_iota(jnp.int32, (c, c), 0)
    s_idx = lax.broadcasted_iota(jnp.int32, (c, c), 1)
    tril = (s_idx <= t_idx).astype(BF16)
    tril2 = jnp.concatenate([tril, tril], axis=1)
    G = [_dot(tril2, jnp.concatenate(_split_hi_lo(g[2]), axis=0)) for g in gates]

    st = [st_sc[h] for h in heads]
    o = [_dot_nt((q[h] * jnp.exp(G[h])).astype(BF16), st[h].astype(BF16)) for h in heads]

    pieces = [[jnp.zeros((m, HGRN_DV), F32)] for _ in heads]
    for blk in range(1, c // m):
        lo_r, hi_r = blk * m, (blk + 1) * m
        scs = []
        for h in heads:
            g_row = G[h][lo_r - 1:lo_r, :]
            qt = q[h][lo_r:hi_r] * jnp.exp(G[h][lo_r:hi_r] - g_row)
            kt = k[h][0:lo_r] * jnp.exp(g_row - G[h][0:lo_r])
            scs.append(_dot_nt(qt.astype(BF16), kt.astype(BF16)))
        for h in heads:
            pieces[h].append(_dot(scs[h].astype(BF16), vb[h][0:lo_r]))
    o = [o[h] + jnp.concatenate(pieces[h], axis=0) for h in heads]

    st_new = []
    for h in heads:
        g_end = G[h][c - 1:c, :]
        kd = k[h] * jnp.exp(g_end - G[h])
        st_new.append(st[h] * jnp.exp(g_end) + _dot_tn(vb[h], kd.astype(BF16)))
    for h in heads:
        st_sc[h] = st_new[h]

    in_blk = lax.broadcasted_iota(jnp.int32, (c, 1), 0) % m
    for d in range(m):
        for h in heads:
            ks = k[h] if d == 0 else pltpu.roll(k[h], d, 0)
            gs = G[h] if d == 0 else pltpu.roll(G[h], d, 0)
            vs = v[h] if d == 0 else pltpu.roll(v[h], d, 0)
            a = jnp.sum(q[h] * ks * jnp.exp(jnp.minimum(G[h] - gs, 0.0)), axis=1, keepdims=True)
            o[h] = o[h] + jnp.where(in_blk >= d, a, 0.0) * vs

    nw = nw_ref[...]
    for h in heads:
        o_ref[:, h * HGRN_DV:(h + 1) * HGRN_DV] = _hgrn_out(o[h], hs(g_ref, h), nw).astype(o_ref.dtype)

    @pl.when(ci == pl.num_programs(2) - 1)
    def _():
        for h in heads:
            s_ref[h] = st_new[h].T


def _hgrn_prompt(h4, lbs_raw, norm_w, batch, t, layer, c=128, m=16, hp=H_C):
    c = min(c, t)
    nc = t // c
    ng = H_C // hp
    w = hp * HGRN_DK
    col = lambda part: (lambda b, g, i: (b * nc + i, part * ng + g))
    return pl.pallas_call(
        functools.partial(_hgrn_prompt_kernel, c=c, m=m, hp=hp, layer=layer),
        grid=(batch, ng, nc),
        in_specs=[pl.BlockSpec((c, w), col(0)), pl.BlockSpec((c, w), col(1)),
                  pl.BlockSpec((c, w), col(2)), pl.BlockSpec((c, w), col(3)),
                  pl.BlockSpec((lbs_raw.shape[0], w), lambda b, g, i: (0, g)),
                  pl.BlockSpec((1, HGRN_DV), lambda b, g, i: (0, 0))],
        out_specs=[pl.BlockSpec((c, w), lambda b, g, i: (b * nc + i, g)),
                   pl.BlockSpec((None, hp, HGRN_DK, HGRN_DV), lambda b, g, i: (b, g, 0, 0))],
        out_shape=[jax.ShapeDtypeStruct((batch * t, D_MODEL), BF16),
                   jax.ShapeDtypeStruct((batch, H_C, HGRN_DK, HGRN_DV), F32)],
        scratch_shapes=[pltpu.VMEM((hp, HGRN_DV, HGRN_DK), F32)],
        compiler_params=_cparams("parallel", "parallel", "arbitrary"),
        name="hgrn_prompt",
    )(h4, h4, h4, h4, lbs_raw, norm_w.reshape(1, HGRN_DV))


def _lane_to_sublane(row):
    n = row.shape[1]
    eye = lax.broadcasted_iota(jnp.int32, (n, n), 0) == lax.broadcasted_iota(jnp.int32, (n, n), 1)
    return jnp.sum(jnp.where(eye, jnp.broadcast_to(row, (n, n)), 0.0), axis=1, keepdims=True)


def _hgrn_decode_kernel(h_ref, s0_ref, lb_ref, nw_ref, o_ref, s_ref, *, layer):
    lb_all = _hgrn_lower_bound(lb_ref, layer)
    nw = nw_ref[...]
    for h in range(H_C):
        sl = lambda part: slice((part * H_C + h) * HGRN_DK, (part * H_C + h + 1) * HGRN_DK)
        q, k, log_f = _hgrn_gates(h_ref[:, sl(0)], h_ref[:, sl(1)], lb_all[:, h * HGRN_DK:(h + 1) * HGRN_DK])
        v = h_ref[:, sl(2)]
        s_new = _lane_to_sublane(jnp.exp(log_f)) * s0_ref[h] + _lane_to_sublane(k) * v
        s_ref[h] = s_new
        o = jnp.sum(_lane_to_sublane(q) * s_new, axis=0, keepdims=True)
        o_ref[:, h * HGRN_DV:(h + 1) * HGRN_DV] = _hgrn_out(o, h_ref[:, sl(3)], nw).astype(o_ref.dtype)


def _hgrn_decode(h4, s0, lbs_raw, norm_w, layer):
    n = h4.shape[0]
    state_spec = pl.BlockSpec((None, H_C, HGRN_DK, HGRN_DV), lambda b: (b, 0, 0, 0))
    o, s = pl.pallas_call(
        functools.partial(_hgrn_decode_kernel, layer=layer),
        grid=(n,),
        in_specs=[pl.BlockSpec((None, 1, 4 * D_MODEL), lambda b: (b, 0, 0)), state_spec,
                  pl.BlockSpec(lbs_raw.shape, lambda b: (0, 0)), pl.BlockSpec((1, HGRN_DV), lambda b: (0, 0))],
        out_specs=[pl.BlockSpec((None, 1, D_MODEL), lambda b: (b, 0, 0)), state_spec],
        out_shape=[jax.ShapeDtypeStruct((n, 1, D_MODEL), BF16), jax.ShapeDtypeStruct(s0.shape, F32)],
        compiler_params=_cparams("parallel"),
        name="hgrn_decode",
    )(h4.reshape(n, 1, 4 * D_MODEL), s0, lbs_raw, norm_w.reshape(1, HGRN_DV))
    return o.reshape(n, D_MODEL), s


MOE_TILE = 512


def _router_kernel(x_ref, w_ref, o_ref):
    logits = _dot(x_ref[...].astype(BF16), w_ref[...])
    lane = lax.broadcasted_iota(jnp.int32, logits.shape, 1)
    lg = jnp.where(lane < N_EXPERTS, logits, -jnp.inf)
    m1 = jnp.max(lg, axis=1, keepdims=True)
    i1 = jnp.min(jnp.where(lg == m1, lane, LANES), axis=1, keepdims=True)
    lg2 = jnp.where(lane == i1, -jnp.inf, lg)
    m2 = jnp.max(lg2, axis=1, keepdims=True)
    i2 = jnp.min(jnp.where(lg2 == m2, lane, LANES), axis=1, keepdims=True)
    e2 = jnp.exp(m2 - m1)
    g1 = 1.0 / (1.0 + e2)
    out = jnp.where(lane == 0, g1, jnp.where(lane == 1, e2 * g1, jnp.where(
        lane == 2, i1.astype(F32), jnp.where(lane == 3, i2.astype(F32), 0.0))))
    o_ref[...] = out


def _router(x, w_router_pad, tm=TOKEN_TILE):
    n = x.shape[0]
    return pl.pallas_call(
        _router_kernel,
        grid=(pl.cdiv(n, tm),),
        in_specs=[pl.BlockSpec((tm, D_MODEL), lambda i: (i, 0)), pl.BlockSpec((D_MODEL, LANES), lambda i: (0, 0))],
        out_specs=pl.BlockSpec((tm, LANES), lambda i: (i, 0)),
        out_shape=jax.ShapeDtypeStruct((n, LANES), F32),
        compiler_params=_cparams("parallel"),
        name="router",
    )(x, w_router_pad)


def _moe_in_kernel(te_ref, nu_ref, xs_ref, wa_ref, wb_ref, o_ref):
    used = pl.program_id(0) < nu_ref[0]

    @pl.when(used)
    def _():
        xb = xs_ref[...]
        a = _dot(xb, wa_ref[...])
        b = _dot(xb, wb_ref[...])
        o_ref[...] = (a * jax.nn.sigmoid(a) * b).astype(o_ref.dtype)

    @pl.when(jnp.logical_not(used))
    def _():
        o_ref[...] = jnp.zeros(o_ref.shape, o_ref.dtype)


def _moe_out_kernel(te_ref, nu_ref, h_ref, w_ref, o_ref):
    used = pl.program_id(0) < nu_ref[0]

    @pl.when(used)
    def _():
        o_ref[...] = _dot(h_ref[...], w_ref[...])

    @pl.when(jnp.logical_not(used))
    def _():
        o_ref[...] = jnp.zeros(o_ref.shape, o_ref.dtype)


def _moe_experts(xs, tile_expert, n_used, w_in, w_out, tn=512):
    p = xs.shape[0]
    f = w_out.shape[1]
    nj = f // tn
    nt = p // MOE_TILE
    h = pl.pallas_call(
        _moe_in_kernel,
        grid_spec=pltpu.PrefetchScalarGridSpec(
            num_scalar_prefetch=2, grid=(nt, nj),
            in_specs=[pl.BlockSpec((MOE_TILE, D_MODEL), lambda i, j, te, nu: (i, 0)),
                      pl.BlockSpec((None, D_MODEL, tn), lambda i, j, te, nu: (te[i], 0, j)),
                      pl.BlockSpec((None, D_MODEL, tn), lambda i, j, te, nu: (te[i], 0, j + nj))],
            out_specs=pl.BlockSpec((MOE_TILE, tn), lambda i, j, te, nu: (i, j))),
        out_shape=jax.ShapeDtypeStruct((p, f), BF16),
        compiler_params=_cparams("parallel", "arbitrary"),
        name="moe_in",
    )(tile_expert, n_used, xs, w_in, w_in)
    return pl.pallas_call(
        _moe_out_kernel,
        grid_spec=pltpu.PrefetchScalarGridSpec(
            num_scalar_prefetch=2, grid=(nt,),
            in_specs=[pl.BlockSpec((MOE_TILE, f), lambda i, te, nu: (i, 0)),
                      pl.BlockSpec((None, f, D_MODEL), lambda i, te, nu: (te[i], 0, 0))],
            out_specs=pl.BlockSpec((MOE_TILE, D_MODEL), lambda i, te, nu: (i, 0))),
        out_shape=jax.ShapeDtypeStruct((p, D_MODEL), F32),
        compiler_params=_cparams("parallel"),
        name="moe_out",
    )(tile_expert, n_used, h, w_out)


def _moe_plan(route):
    n = route.shape[0]
    e_idx = route[:, 2:4].astype(jnp.int32).reshape(-1)
    onehot = (e_idx[:, None] == jnp.arange(N_EXPERTS, dtype=jnp.int32)[None, :]).astype(jnp.int32)
    running = jnp.cumsum(onehot, axis=0)
    rank = jnp.take_along_axis(running, e_idx[:, None], axis=1)[:, 0] - 1
    counts = running[-1]
    tiles = (counts + MOE_TILE - 1) // MOE_TILE
    tile_end = jnp.cumsum(tiles)
    tile_start = tile_end - tiles
    nt = (2 * n + N_EXPERTS * (MOE_TILE - 1) + MOE_TILE - 1) // MOE_TILE
    dest = tile_start[e_idx] * MOE_TILE + rank
    token = jnp.zeros((nt * MOE_TILE,), jnp.int32).at[dest].set(jnp.arange(2 * n, dtype=jnp.int32) // 2)
    past_end = (jnp.arange(nt, dtype=jnp.int32)[:, None] >= tile_end[None, :]).astype(jnp.int32)
    tile_expert = jnp.minimum(jnp.sum(past_end, axis=1), N_EXPERTS - 1).astype(jnp.int32)
    return token, dest.reshape(n, 2), tile_expert, tile_end[-1:].astype(jnp.int32)


def _moe(x1, w_router_pad, w_in, w_out):
    route = _router(x1, w_router_pad)
    token, pos, tile_expert, n_used = _moe_plan(route)
    xs = x1.astype(BF16)[token]
    ys = _moe_experts(xs, tile_expert, n_used, w_in, w_out)
    return ys[pos[:, 0]], ys[pos[:, 1]], route


def kernel(x_prompt, x_sample, cache_attn, state_hgrn, page_table, p_prompt, p_sample, w_in_even, w_out_even,
           w_in_odd, w_out_odd, hgrn_lower_bounds, hgrn_norm_w, ln_mix_g, ln_mix_b, ln_ffn_g, ln_ffn_b,
           w_ffn_in, w_ffn_out, w_router, w_moe_in, w_moe_out, w_ple_proj, w_ple_gate):
    batch, t, d = x_prompt.shape
    n_seq = x_sample.shape[0]
    n = batch * t
    past_len = page_table.shape[1] * PAGE_SIZE
    x = jnp.concatenate([x_prompt.reshape(n, d), x_sample.reshape(n_seq, d)], axis=0)
    pos = jnp.concatenate([jnp.tile(jnp.arange(t, dtype=jnp.int32), batch), jnp.full((n_seq,), past_len, jnp.int32)])
    cs = _rotary_table(pos)
    p_all = jnp.concatenate([p_prompt.reshape(DEPTH, n, PLE_DIM), p_sample.reshape(DEPTH, n_seq, PLE_DIM)], axis=1)
    join = lambda a, b: jnp.concatenate([a, b], axis=0)
    cache_t = jnp.swapaxes(cache_attn, 2, 3)

    rows_p, rows_s, state_p, state_s = [], [], [], []
    for i in range(DEPTH):
        j = i // 2
        ple = (p_all[i], w_ple_gate[i].astype(BF16), w_ple_proj[i].astype(BF16))
        if i % 2 == 0:
            w_rows, w_q = _prep_even_in(w_in_even[j])
            rows, qa, qi, wi, qb, ka, va, ki, kb, vb = _even_inproj(x, w_rows, w_q, cs)
            oa_p = _dsa_prompt(qa, qi, wi, ka, va, ki, batch, t)
            ob_p = _sb_prompt(qb, kb, vb, batch, t)
            oa_s, ob_s = _even_decode(page_table, cache_t, j, qa[n:], qi[n:], wi[n:], qb[n:], rows[n:])
            w_o = w_out_even[j].astype(BF16)
            x1 = _mix_ln(x, ln_mix_g[i], ln_mix_b[i],
                         mm=[(join(oa_p, oa_s), w_o[:WIDTH_A]), (join(ob_p, ob_s), w_o[WIDTH_A:])])
            h = _swiglu_in(x1, w_ffn_in[j].astype(BF16))
            x = _mix_ln(x1, ln_ffn_g[i], ln_ffn_b[i], mm=[(h, w_ffn_out[j].astype(BF16))], ple=ple)
            rows_p.append(rows[:n].reshape(batch, t, ROW_DIM))
            rows_s.append(rows[n:].reshape(n_seq, 1, ROW_DIM))
        else:
            h4 = _matmul(x, w_in_odd[j].astype(BF16))
            o_p, s_p = _hgrn_prompt(h4, hgrn_lower_bounds, hgrn_norm_w[j], batch, t, j)
            o_s, s_s = _hgrn_decode(h4[n:], state_hgrn[j], hgrn_lower_bounds, hgrn_norm_w[j], j)
            x1 = _mix_ln(x, ln_mix_g[i], ln_mix_b[i], mm=[(join(o_p, o_s), w_out_odd[j].astype(BF16))])
            w_r = jnp.pad(w_router[j], ((0, 0), (0, LANES - N_EXPERTS))).astype(BF16)
            y1, y2, route = _moe(x1, w_r, w_moe_in[j].astype(BF16), w_moe_out[j].astype(BF16))
            x = _mix_ln(x1, ln_ffn_g[i], ln_ffn_b[i], comb=(y1, y2), gates=route, ple=ple)
            state_p.append(s_p)
            state_s.append(s_s)
    return (x[:n].reshape(batch, t, d), x[n:].reshape(n_seq, 1, d), jnp.stack(rows_p), jnp.stack(rows_s),
            jnp.stack(state_p), jnp.stack(state_s))
```

```python
import functools

import jax
import jax.numpy as jnp
import numpy as np
from jax import lax
from jax.experimental import pallas as pl
from jax.experimental.pallas import tpu as pltpu

D_MODEL = 1024
DEPTH = 4
PAGE_SIZE = 128
HEAD_DIM = 64
N_HEADS_A = 8
N_HEADS_B = 8
IDX_HEADS = 4
IDX_DIM = 64
TOPK_MAX = 256
ROPE_THETA = 500000.0
H_C = 8
HGRN_DK = 128
HGRN_DV = D_MODEL // H_C
D_FF = 2816
N_EXPERTS = 8
D_FF_EXPERT = 3584
PLE_DIM = 256
DEEPNORM_ALPHA = (2 * DEPTH) ** 0.25
LN_EPS = 1e-5
RMS_EPS = 1e-6
NEG_BIG = -1e30
WIDTH_A = N_HEADS_A * HEAD_DIM
WIDTH_B = N_HEADS_B * HEAD_DIM
ROW_DIM = 2 * WIDTH_A + IDX_DIM + 2 * WIDTH_B

LANES = 128
HALF = LANES // 2
VMEM_LIMIT = 56 * 1024 * 1024
TOKEN_TILE = 384
INT_MIN = -2 ** 31
NO_INDEX = 2 ** 30
EXP_IS_ZERO = -104.0

BF16 = jnp.bfloat16
F32 = jnp.float32


def _cparams(*sem, vmem=VMEM_LIMIT):
    return pltpu.CompilerParams(dimension_semantics=sem, vmem_limit_bytes=vmem)


def _dot(a, b):
    return jnp.dot(a, b, preferred_element_type=F32)


def _dot_nt(a, b):
    return lax.dot_general(a, b, (((1,), (1,)), ((), ())), preferred_element_type=F32)


def _dot_tn(a, b):
    return lax.dot_general(a, b, (((0,), (0,)), ((), ())), preferred_element_type=F32)


def _low_half(shape):
    return lax.broadcasted_iota(jnp.int32, shape, len(shape) - 1) < HALF


def _split_hi_lo(x):
    hi = x.astype(BF16)
    lo = (x - hi.astype(F32)).astype(BF16)
    return hi, lo


def _rotary(t, c, s1, s2):
    return t * c + pltpu.roll(t, LANES - 8, 1) * s1 + pltpu.roll(t, 8, 1) * s2


def _even_inproj_kernel(x_ref, wr_ref, wq_ref, cs_ref, rows_ref, qa_ref, qi_ref, wi_ref, qb_ref,
                        ka_ref, va_ref, ki_ref, kb_ref, vb_ref):
    xb = x_ref[...].astype(BF16)
    c = cs_ref[:, 0:LANES]
    s1 = cs_ref[:, LANES:2 * LANES]
    s2 = cs_ref[:, 2 * LANES:3 * LANES]
    low = _low_half(c.shape)
    acc = _dot(xb, wr_ref[...])
    for t in range(WIDTH_A // LANES):
        sl = slice(t * LANES, (t + 1) * LANES)
        r = _rotary(acc[:, sl], c, s1, s2)
        rows_ref[:, sl] = r
        ka_ref[:, sl] = r.astype(BF16)
    v_a = acc[:, WIDTH_A:2 * WIDTH_A]
    rows_ref[:, WIDTH_A:2 * WIDTH_A] = v_a
    va_ref[...] = v_a.astype(BF16)
    o_i = 2 * WIDTH_A
    t8 = acc[:, o_i:o_i + LANES]
    r8 = _rotary(t8, jnp.where(low, c, 1.0), jnp.where(low, s1, 0.0), jnp.where(low, s2, 0.0))
    rows_ref[:, o_i:o_i + LANES] = r8
    ki_ref[...] = jnp.where(low, r8, pltpu.roll(r8, HALF, 1)).astype(BF16)
    rows_ref[:, o_i + LANES:ROW_DIM] = acc[:, o_i + LANES:ROW_DIM]
    o_b = o_i + IDX_DIM
    kb_ref[...] = acc[:, o_b:o_b + WIDTH_B].astype(BF16)
    vb_ref[...] = acc[:, o_b + WIDTH_B:o_b + 2 * WIDTH_B].astype(BF16)

    accq = _dot(xb, wq_ref[...])
    scale = HEAD_DIM ** -0.5

    def put_pairs(dst_ref, src_off, n_pairs, rot):
        for p in range(n_pairs):
            t = accq[:, src_off + p * LANES:src_off + (p + 1) * LANES]
            if rot:
                t = _rotary(t, c, s1, s2)
            t = t * scale
            dst_ref[:, (2 * p) * LANES:(2 * p + 1) * LANES] = jnp.where(low, t, 0.0).astype(BF16)
            dst_ref[:, (2 * p + 1) * LANES:(2 * p + 2) * LANES] = jnp.where(low, 0.0, t).astype(BF16)

    put_pairs(qa_ref, 0, N_HEADS_A // 2, True)
    put_pairs(qi_ref, WIDTH_A, IDX_HEADS // 2, True)
    put_pairs(qb_ref, WIDTH_A + IDX_HEADS * IDX_DIM, N_HEADS_B // 2, False)
    o_w = WIDTH_A + IDX_HEADS * IDX_DIM + WIDTH_B
    wi_ref[...] = accq[:, o_w:o_w + LANES] * IDX_HEADS ** -0.5


def _even_inproj(x, w_rows, w_q, cs, tm=TOKEN_TILE):
    n = x.shape[0]
    qw = w_q.shape[1]
    row = lambda i: (i, 0)
    const = lambda i: (0, 0)
    outs = [
        (ROW_DIM, F32), (2 * WIDTH_A, BF16), (2 * IDX_HEADS * IDX_DIM, BF16), (LANES, F32), (2 * WIDTH_B, BF16),
        (WIDTH_A, BF16), (WIDTH_A, BF16), (LANES, BF16), (WIDTH_B, BF16), (WIDTH_B, BF16),
    ]
    return pl.pallas_call(
        _even_inproj_kernel,
        grid=(pl.cdiv(n, tm),),
        in_specs=[pl.BlockSpec((tm, D_MODEL), row), pl.BlockSpec((D_MODEL, ROW_DIM), const),
                  pl.BlockSpec((D_MODEL, qw), const), pl.BlockSpec((tm, 3 * LANES), row)],
        out_specs=[pl.BlockSpec((tm, w), row) for w, _ in outs],
        out_shape=[jax.ShapeDtypeStruct((n, w), d) for w, d in outs],
        compiler_params=_cparams("parallel"),
        name="even_inproj",
    )(x, w_rows, w_q, cs)


def _matmul_kernel(x_ref, w_ref, o_ref):
    o_ref[...] = _dot(x_ref[...].astype(BF16), w_ref[...]).astype(o_ref.dtype)


def _matmul(x, w, out_dtype=F32, tm=TOKEN_TILE, tn=1024):
    n, k = x.shape
    m = w.shape[1]
    tn = min(tn, m)
    return pl.pallas_call(
        _matmul_kernel,
        grid=(m // tn, pl.cdiv(n, tm)),
        in_specs=[pl.BlockSpec((tm, k), lambda j, i: (i, 0)), pl.BlockSpec((k, tn), lambda j, i: (0, j))],
        out_specs=pl.BlockSpec((tm, tn), lambda j, i: (i, j)),
        out_shape=jax.ShapeDtypeStruct((n, m), out_dtype),
        compiler_params=_cparams("parallel", "arbitrary"),
        name="matmul",
    )(x, w)


def _swiglu_in_kernel(x_ref, wa_ref, wb_ref, o_ref):
    xb = x_ref[...].astype(BF16)
    a = _dot(xb, wa_ref[...])
    b = _dot(xb, wb_ref[...])
    o_ref[...] = (a * jax.nn.sigmoid(a) * b).astype(o_ref.dtype)


def _swiglu_in(x, w_in, tm=TOKEN_TILE, tn=1408):
    n, k = x.shape
    f = w_in.shape[1] // 2
    nj = f // tn
    return pl.pallas_call(
        _swiglu_in_kernel,
        grid=(nj, pl.cdiv(n, tm)),
        in_specs=[pl.BlockSpec((tm, k), lambda j, i: (i, 0)),
                  pl.BlockSpec((k, tn), lambda j, i: (0, j)),
                  pl.BlockSpec((k, tn), lambda j, i: (0, j + nj))],
        out_specs=pl.BlockSpec((tm, tn), lambda j, i: (i, j)),
        out_shape=jax.ShapeDtypeStruct((n, f), BF16),
        compiler_params=_cparams("parallel", "arbitrary"),
        name="swiglu_in",
    )(x, w_in, w_in)


def _layernorm(y, g, b):
    mu = jnp.mean(y, axis=-1, keepdims=True)
    d = y - mu
    var = jnp.mean(d * d, axis=-1, keepdims=True)
    return d * lax.rsqrt(var + LN_EPS) * g + b


def _mix_ln_kernel(n_mm, n_comb, ple, *refs):
    it = iter(refs)
    mm = [(next(it), next(it)) for _ in range(n_mm)]
    comb = [next(it) for _ in range(n_comb)]
    gates_ref = next(it) if n_comb else None
    x_ref, g_ref, b_ref = next(it), next(it), next(it)
    if ple:
        p_ref, wg_ref, wp_ref = next(it), next(it), next(it)
    o_ref = next(it)
    y = DEEPNORM_ALPHA * x_ref[...]
    for h_ref, w_ref in mm:
        y = y + _dot(h_ref[...], w_ref[...])
    for k, y_ref in enumerate(comb):
        y = y + gates_ref[:, k:k + 1] * y_ref[...]
    y = _layernorm(y, g_ref[...], b_ref[...])
    if ple:
        gate = jax.nn.sigmoid(_dot(y.astype(BF16), wg_ref[...]))
        y = y + gate * _dot(p_ref[...].astype(BF16), wp_ref[...])
    o_ref[...] = y


def _mix_ln(x, g, b, mm=(), comb=(), gates=None, ple=None, tm=TOKEN_TILE):
    n = x.shape[0]
    row = lambda i: (i, 0)
    const = lambda i: (0, 0)
    args, specs = [], []
    for h, w in mm:
        args += [h, w]
        specs += [pl.BlockSpec((tm, h.shape[1]), row), pl.BlockSpec(w.shape, const)]
    for y in comb:
        args.append(y)
        specs.append(pl.BlockSpec((tm, D_MODEL), row))
    if comb:
        args.append(gates)
        specs.append(pl.BlockSpec((tm, gates.shape[1]), row))
    args += [x, g.reshape(1, D_MODEL), b.reshape(1, D_MODEL)]
    specs += [pl.BlockSpec((tm, D_MODEL), row), pl.BlockSpec((1, D_MODEL), const), pl.BlockSpec((1, D_MODEL), const)]
    if ple is not None:
        p, wg, wp = ple
        args += [p, wg, wp]
        specs += [pl.BlockSpec((tm, PLE_DIM), row), pl.BlockSpec(wg.shape, const), pl.BlockSpec(wp.shape, const)]
    return pl.pallas_call(
        functools.partial(_mix_ln_kernel, len(mm), len(comb), ple is not None),
        grid=(pl.cdiv(n, tm),),
        in_specs=specs,
        out_specs=pl.BlockSpec((tm, D_MODEL), row),
        out_shape=jax.ShapeDtypeStruct((n, D_MODEL), F32),
        compiler_params=_cparams("parallel"),
        name="mix_ln",
    )(*args)


def _prep_even_in(w):
    sizes = (WIDTH_A, WIDTH_A, WIDTH_A, IDX_HEADS * IDX_DIM, IDX_DIM, IDX_HEADS, WIDTH_B, WIDTH_B, WIDTH_B)
    qa, ka, va, qi, ki, wi, qb, kb, vb = jnp.split(w, np.cumsum(sizes)[:-1].tolist(), axis=1)
    w_rows = jnp.concatenate([ka, va, ki, kb, vb], axis=1)
    pad = jnp.zeros((w.shape[0], LANES - IDX_HEADS), w.dtype)
    w_q = jnp.concatenate([qa, qi, qb, wi, pad], axis=1)
    return w_rows.astype(BF16), w_q.astype(BF16)


def _rotary_table(pos):
    rd = HEAD_DIM // 4
    half = rd // 2
    inv_freq = ROPE_THETA ** (-jnp.arange(half, dtype=F32) * 2.0 / rd)
    ang = pos.astype(F32)[:, None] * inv_freq[None, :]
    cos, sin = jnp.cos(ang), jnp.sin(ang)
    n = pos.shape[0]
    ones = jnp.ones((n, HEAD_DIM - rd), F32)
    zeros = jnp.zeros((n, HEAD_DIM - rd), F32)
    zh = jnp.zeros((n, half), F32)
    c = jnp.concatenate([cos, cos, ones], axis=1)
    s1 = jnp.concatenate([-sin, zh, zeros], axis=1)
    s2 = jnp.concatenate([zh, sin, zeros], axis=1)
    return jnp.concatenate([c, c, s1, s1, s2, s2], axis=1)


def _softplus(z):
    return jnp.maximum(z, 0.0) + jnp.log(1.0 + jnp.exp(-jnp.abs(z)))


def _sb_tiles(qs, kvs, upper2, state, valid):
    work = [(c, kt, vt) for kt, vt in kvs for c in range(len(qs))]
    zs = [_dot_nt(qs[c], kt) for c, kt, _ in work]
    sps = [_softplus(z) for z in zs]
    sp_ms = sps if valid is None else [jnp.where(valid, sp, 0.0) for sp in sps]
    sums = [_dot(jnp.concatenate(_split_hi_lo(sp_m), axis=1), upper2) for sp_m in sp_ms]
    carries = [carry for carry, _ in state]
    ws = []
    for (c, _, _), z, sp, sp_m, x in zip(work, zs, sps, sp_ms, sums):
        w = jnp.exp(z - sp + (carries[c] - x))
        ws.append((w if valid is None else jnp.where(valid, w, 0.0)).astype(BF16))
        carries[c] = carries[c] - jnp.sum(sp_m, axis=1, keepdims=True)
    accs = [acc for _, acc in state]
    for (c, _, vt), w in zip(work, ws):
        accs[c] = accs[c] + _dot(w, vt)
    return tuple(zip(carries, accs))


def _sb_prompt_kernel(q_ref, k_ref, v_ref, o_ref, *, tq, unroll):
    qi = pl.program_id(2)
    rows = lax.broadcasted_iota(jnp.int32, (tq, tq), 0)
    cols = lax.broadcasted_iota(jnp.int32, (tq, tq), 1)
    upper = (rows > cols).astype(BF16)
    upper2 = jnp.concatenate([upper, upper], axis=0)
    qs = [q_ref[:, 0:LANES], q_ref[:, LANES:2 * LANES]]

    def tiles(j_first, n, state, valid):
        kvs = []
        for u in range(n):
            start = pl.multiple_of((j_first - u) * tq, tq)
            kvs.append((k_ref[pl.ds(start, tq), :], v_ref[pl.ds(start, tq), :]))
        return _sb_tiles(qs, kvs, upper2, state, valid)

    zero = (jnp.zeros((tq, 1), F32), jnp.zeros((tq, LANES), F32))
    state = tiles(qi, 1, (zero, zero), cols < rows)

    def live(st):
        return jnp.max(jnp.maximum(st[0][0], st[1][0])) > EXP_IS_ZERO

    n_left = qi % unroll
    for u in range(1, unroll):
        state = lax.cond(n_left >= u, lambda st, u=u: tiles(qi - u, 1, st, None), lambda st: st, state)
    first = qi - 1 - n_left

    def step(c):
        jj, st, _ = c
        st = tiles(first - unroll * jj, unroll, st, None)
        return jj + 1, st, live(st)

    _, state, _ = lax.while_loop(lambda c: jnp.logical_and(c[0] < qi // unroll, c[2]), step,
                                 (jnp.int32(0), state, live(state)))
    low = _low_half((tq, LANES))
    o_ref[...] = jnp.where(low, state[0][1], state[1][1]).astype(o_ref.dtype)


def _sb_prompt(q_pad, k, v, batch, t, tq=256, unroll=1):
    tq = min(tq, t)
    nq = t // tq
    return pl.pallas_call(
        functools.partial(_sb_prompt_kernel, tq=tq, unroll=unroll),
        grid=(batch, N_HEADS_B // 2, nq),
        in_specs=[pl.BlockSpec((tq, 2 * LANES), lambda b, p, i: (b * nq + i, p)),
                  pl.BlockSpec((t, LANES), lambda b, p, i: (b, p)),
                  pl.BlockSpec((t, LANES), lambda b, p, i: (b, p))],
        out_specs=pl.BlockSpec((tq, LANES), lambda b, p, i: (b * nq + i, p)),
        out_shape=jax.ShapeDtypeStruct((batch * t, WIDTH_B), BF16),
        compiler_params=_cparams("parallel", "parallel", "arbitrary"),
        name="sb_prompt",
    )(q_pad, k, v)


def _sortable(x):
    b = lax.bitcast_convert_type(x + 0.0, jnp.int32)
    return b ^ ((b >> 31) & 0x7FFFFFFF)


def _dsa_prompt_kernel(qa_ref, qi_ref, wi_ref, ka_ref, va_ref, ki_ref, o_ref, key_sc, bias_sc, mx_sc, acc_sc,
                       *, tq, tk, n_sel, t):
    q0 = pl.program_id(1) * tq
    nkt = (q0 + tq + tk - 1) // tk
    rows = q0 + lax.broadcasted_iota(jnp.int32, (tq, tk), 0)
    cols0 = lax.broadcasted_iota(jnp.int32, (tq, tk), 1)
    w = wi_ref[...]

    def sweep(body):
        def pair(jj, carry):
            body([pl.multiple_of(2 * jj * tk, tk), pl.multiple_of((2 * jj + 1) * tk, tk)])
            return carry

        lax.fori_loop(0, nkt // 2, pair, 0)

        @pl.when(nkt % 2 == 1)
        def _():
            body([pl.multiple_of((nkt - 1) * tk, tk)])

    def score_tiles(starts):
        ss = [[_dot_nt(qi_ref[:, h * LANES:(h + 1) * LANES], ki_ref[pl.ds(start, tk), :])
               for h in range(IDX_HEADS)] for start in starts]
        for start, s in zip(starts, ss):
            score = jnp.zeros((tq, tk), F32)
            for h in range(IDX_HEADS):
                score = score + jnp.maximum(s[h], 0.0) * w[:, h:h + 1]
            score = jnp.where(start + cols0 <= rows, score, NEG_BIG)
            key_sc[:, pl.ds(start, tk)] = _sortable(score)

    sweep(score_tiles)

    def count(pred):
        def body(j, c):
            start = pl.multiple_of(j * tk, tk)
            hit = jnp.where(pred(key_sc[:, pl.ds(start, tk)], start + cols0), 1.0, 0.0)
            for u in range(tk // LANES):
                c = c + hit[:, u * LANES:(u + 1) * LANES]
            return c
        c = lax.fori_loop(0, nkt, body, jnp.zeros((tq, LANES), F32))
        return jnp.sum(c, axis=1, keepdims=True)

    def bit_step(i, v):
        cand = v ^ lax.shift_left(jnp.int32(1), 31 - i)
        return jnp.where(count(lambda kk, kp: kk >= cand) >= n_sel, cand, v)

    v = lax.fori_loop(0, 32, bit_step, jnp.full((tq, 1), INT_MIN, jnp.int32))
    r = n_sel - count(lambda kk, kp: kk > v)
    n_eq = count(lambda kk, kp: kk == v)

    def tie_cut():
        nbits = (t - 1).bit_length()

        def step(i, c):
            cand = c + lax.shift_left(jnp.int32(1), nbits - 1 - i)
            g = count(lambda kk, kp: jnp.where(kk == v, kp, NO_INDEX) < cand)
            return jnp.where(g < r, cand, c)

        return lax.fori_loop(0, nbits, step, jnp.zeros((tq, 1), jnp.int32))

    cut = lax.cond(jnp.max(n_eq - r) > 0, tie_cut, lambda: jnp.full((tq, 1), t, jnp.int32))

    heads = range(N_HEADS_A)

    def logits(start):
        kts = [ka_ref[pl.ds(start, tk), p * LANES:(p + 1) * LANES] for p in range(N_HEADS_A // 2)]
        return [_dot_nt(qa_ref[:, h * LANES:(h + 1) * LANES], kts[h // 2]) for h in heads]

    mx_sc[...] = jnp.full(mx_sc.shape, NEG_BIG, F32)

    def max_tiles(starts):
        lgs = [logits(start) for start in starts]
        biases = []
        for start in starts:
            kk = key_sc[:, pl.ds(start, tk)]
            kp = start + cols0
            tie_ok = jnp.where(kk == v, kp, NO_INDEX) <= cut
            bias = jnp.where(kp <= rows, jnp.where(kk > v, 0.0, jnp.where(tie_ok, 0.0, NEG_BIG)), NEG_BIG)
            bias_sc[:, pl.ds(start, tk)] = bias
            biases.append(bias)
        for h in heads:
            mx = mx_sc[h]
            for lg, bias in zip(lgs, biases):
                mx = jnp.maximum(mx, lg[h] + bias)
            mx_sc[h] = mx

    sweep(max_tiles)
    m = [jnp.max(mx_sc[h], axis=1, keepdims=True) for h in heads]
    mx_sc[...] = jnp.zeros(mx_sc.shape, F32)
    acc_sc[...] = jnp.zeros(acc_sc.shape, F32)

    def sum_tiles(starts):
        lgs = [logits(start) for start in starts]
        prs = []
        for start, lg in zip(starts, lgs):
            bias = bias_sc[:, pl.ds(start, tk)]
            prs.append([jnp.exp(lg[h] + bias - m[h]) for h in heads])
        for h in heads:
            total = mx_sc[h]
            for pr in prs:
                total = total + pr[h]
            mx_sc[h] = total
        for h in heads:
            acc = acc_sc[h]
            for start, pr in zip(starts, prs):
                acc = acc + _dot(pr[h].astype(BF16), va_ref[pl.ds(start, tk), (h // 2) * LANES:(h // 2 + 1) * LANES])
            acc_sc[h] = acc

    sweep(sum_tiles)
    low = _low_half((tq, LANES))
    outs = [acc_sc[h] / jnp.sum(mx_sc[h], axis=1, keepdims=True) for h in heads]
    for p in range(N_HEADS_A // 2):
        o_ref[:, p * LANES:(p + 1) * LANES] = jnp.where(low, outs[2 * p], outs[2 * p + 1]).astype(o_ref.dtype)


def _dsa_prompt(qa_pad, qi_pad, wi, ka, va, ki2, batch, t, tq=128, tk=512):
    tq = min(tq, t)
    tk = min(tk, t)
    nq = t // tq
    n_sel = max(1, min(TOPK_MAX, t // 4))
    qrow = lambda b, i: (b * nq + i, 0)
    whole_seq = lambda w: pl.BlockSpec((t, w), lambda b, i: (b, 0), pipeline_mode=pl.Buffered(1))
    return pl.pallas_call(
        functools.partial(_dsa_prompt_kernel, tq=tq, tk=tk, n_sel=n_sel, t=t),
        grid=(batch, nq),
        in_specs=[pl.BlockSpec((tq, 2 * WIDTH_A), qrow), pl.BlockSpec((tq, 2 * IDX_HEADS * IDX_DIM), qrow),
                  pl.BlockSpec((tq, LANES), qrow), whole_seq(WIDTH_A), whole_seq(WIDTH_A), whole_seq(LANES)],
        out_specs=pl.BlockSpec((tq, WIDTH_A), qrow),
        out_shape=jax.ShapeDtypeStruct((batch * t, WIDTH_A), BF16),
        scratch_shapes=[pltpu.VMEM((tq, t), jnp.int32), pltpu.VMEM((tq, t), F32),
                        pltpu.VMEM((N_HEADS_A, tq, tk), F32), pltpu.VMEM((N_HEADS_A, tq, LANES), F32)],
        compiler_params=_cparams("parallel", "arbitrary"),
        name="dsa_prompt",
    )(qa_pad, qi_pad, wi, ka, va, ki2)


OFF_KI = 2 * WIDTH_A
OFF_KB = OFF_KI + IDX_DIM
OFF_VB = OFF_KB + WIDTH_B


def _head_rows(row, off, n_rows=8):
    w = row.shape[1]
    col = lax.broadcasted_iota(jnp.int32, (n_rows, w), 1)
    hrow = lax.broadcasted_iota(jnp.int32, (n_rows, w), 0)
    head_of_col = ((col - off + HEAD_DIM) >> 6) - 1
    return jnp.where(head_of_col == hrow, jnp.broadcast_to(row, (n_rows, w)), 0.0)


def _sum_all(x):
    return jnp.sum(jnp.sum(x, axis=0, keepdims=True), axis=1, keepdims=True)


def _even_decode_kernel(pt_ref, qa_ref, qi_ref, wi_ref, qb_ref, self_ref, *rest, n_pages, n_sel):
    pages = rest[:n_pages]
    oa_ref, ob_ref, key_sc, la_sc, z_sc = rest[n_pages:]
    n_past = n_pages * PAGE_SIZE
    f32_tile = lambda ref, h: ref[:, h * LANES:(h + 1) * LANES].astype(F32)
    swap = lambda t: pltpu.roll(t, HALF, 1)

    compact = lambda ref, n_heads: jnp.concatenate(
        [f32_tile(ref, 2 * p) + f32_tile(ref, 2 * p + 1) for p in range(n_heads // 2)], axis=1)
    qa_bd16 = _head_rows(compact(qa_ref, N_HEADS_A), 0).astype(BF16)
    qb_bd16 = _head_rows(compact(qb_ref, N_HEADS_B), 0).astype(BF16)
    head_mask = _head_rows(jnp.ones((1, WIDTH_A), F32), 0)
    page16 = lambda s, off, width: pages[s][off:off + width, :].astype(BF16)
    hrow = lax.broadcasted_iota(jnp.int32, (8, LANES), 0)
    lane8 = lax.broadcasted_iota(jnp.int32, (8, LANES), 1)
    qi_rows = jnp.zeros((8, LANES), F32)
    for h in range(IDX_HEADS):
        t = f32_tile(qi_ref, h)
        t = swap(t) if h % 2 else t
        qi_rows = jnp.where(hrow == h, jnp.broadcast_to(t, (8, LANES)), qi_rows)
    qi_rows16 = qi_rows.astype(BF16)
    w_col = jnp.sum(jnp.where(lane8 == hrow, jnp.broadcast_to(wi_ref[...], (8, LANES)), 0.0), axis=1, keepdims=True)

    def idx_score(s_i):
        return jnp.sum(jnp.maximum(s_i, 0.0) * w_col, axis=0, keepdims=True)

    for s in range(n_pages):
        key_sc[s:s + 1, :] = _sortable(idx_score(_dot(qi_rows16, page16(s, OFF_KI, LANES))))
        la_sc[s] = _dot(qa_bd16, page16(s, 0, WIDTH_A))
        z_sc[s] = _dot(qb_bd16, page16(s, OFF_KB, WIDTH_B))

    rnd = lambda x: x.astype(BF16).astype(F32)
    s_self = jnp.sum(qi_rows16.astype(F32) * rnd(self_ref[:, OFF_KI:OFF_KI + LANES]), axis=1, keepdims=True)
    sc_self = idx_score(s_self)
    la_self = jnp.sum(qa_bd16.astype(F32) * rnd(self_ref[:, 0:WIDTH_A]), axis=1, keepdims=True)
    lane1 = lax.broadcasted_iota(jnp.int32, (1, LANES), 1)
    key_sc[n_pages:n_pages + 1, :] = jnp.where(lane1 == 0, _sortable(jnp.broadcast_to(sc_self, (1, LANES))), INT_MIN)

    keys = key_sc[...]
    kp = (lax.broadcasted_iota(jnp.int32, keys.shape, 0) * PAGE_SIZE
          + lax.broadcasted_iota(jnp.int32, keys.shape, 1))
    count = lambda pred: _sum_all(jnp.where(pred, 1, 0))

    def bit_step(i, v):
        cand = v ^ lax.shift_left(jnp.int32(1), 31 - i)
        return jnp.where(count(keys >= cand) >= n_sel, cand, v)

    v = lax.fori_loop(0, 32, bit_step, jnp.full((1, 1), INT_MIN, jnp.int32))
    r = n_sel - count(keys > v)
    n_eq = count(keys == v)
    tie_pos = jnp.where(keys == v, kp, NO_INDEX)

    def tie_cut():
        nbits = n_past.bit_length()

        def step(i, c):
            cand = c + lax.shift_left(jnp.int32(1), nbits - 1 - i)
            return jnp.where(count(tie_pos < cand) < r, cand, c)

        return lax.fori_loop(0, nbits, step, jnp.zeros((1, 1), jnp.int32))

    cut = lax.cond(jnp.max(n_eq - r) > 0, tie_cut, lambda: jnp.full((1, 1), NO_INDEX - 1, jnp.int32))
    bias = jnp.where(kp <= n_past, jnp.where(keys > v, 0.0, jnp.where(tie_pos <= cut, 0.0, NEG_BIG)), NEG_BIG)

    bias_self = bias[n_pages:n_pages + 1, 0:1]
    m = la_self + bias_self
    for s in range(n_pages):
        m = jnp.maximum(m, jnp.max(la_sc[s] + bias[s:s + 1, :], axis=1, keepdims=True))
    p_self = jnp.exp(la_self + bias_self - m)
    l = p_self
    acc = rnd(p_self) * rnd(self_ref[:, WIDTH_A:2 * WIDTH_A])
    for s in range(n_pages):
        pr = jnp.exp(la_sc[s] + bias[s:s + 1, :] - m)
        l = l + jnp.sum(pr, axis=1, keepdims=True)
        acc = acc + _dot_nt(pr.astype(BF16), page16(s, WIDTH_A, WIDTH_A))
    oa_ref[...] = jnp.sum(head_mask * (acc / l), axis=0, keepdims=True).astype(oa_ref.dtype)

    t_idx = lax.broadcasted_iota(jnp.int32, (PAGE_SIZE, PAGE_SIZE), 0)
    s_idx = lax.broadcasted_iota(jnp.int32, (PAGE_SIZE, PAGE_SIZE), 1)
    upper = (t_idx > s_idx).astype(BF16)
    carry = jnp.zeros((8, 1), F32)
    acc_b = jnp.zeros((8, WIDTH_B), F32)
    for s in reversed(range(n_pages)):
        z = z_sc[s]
        sp = _softplus(z)
        hi, lo = _split_hi_lo(-sp)
        later = _dot(hi, upper) + _dot(lo, upper) + carry
        w = jnp.exp(z - sp + later)
        acc_b = acc_b + _dot_nt(w.astype(BF16), page16(s, OFF_VB, WIDTH_B))
        carry = carry - jnp.sum(sp, axis=1, keepdims=True)
    ob_ref[...] = jnp.sum(head_mask * acc_b, axis=0, keepdims=True).astype(ob_ref.dtype)


def _even_decode(page_table, cache_t, layer, qa_pad, qi_pad, wi, qb_pad, self_rows):
    n, n_pages = page_table.shape
    n_sel = max(1, min(TOPK_MAX, (n_pages * PAGE_SIZE + 1) // 4))
    per_seq = lambda w: pl.BlockSpec((None, 1, w), lambda b, pt: (b, 0, 0))
    page_spec = lambda s: pl.BlockSpec((None, None, ROW_DIM, PAGE_SIZE), lambda b, pt: (layer, pt[b, s], 0, 0))
    r3 = lambda a: a.reshape(n, 1, a.shape[-1])
    oa, ob = pl.pallas_call(
        functools.partial(_even_decode_kernel, n_pages=n_pages, n_sel=n_sel),
        grid_spec=pltpu.PrefetchScalarGridSpec(
            num_scalar_prefetch=1, grid=(n,),
            in_specs=[per_seq(2 * WIDTH_A), per_seq(2 * IDX_HEADS * IDX_DIM), per_seq(LANES), per_seq(2 * WIDTH_B),
                      per_seq(ROW_DIM)] + [page_spec(s) for s in range(n_pages)],
            out_specs=[per_seq(WIDTH_A), per_seq(WIDTH_B)],
            scratch_shapes=[pltpu.VMEM((n_pages + 1, LANES), jnp.int32), pltpu.VMEM((n_pages, 8, LANES), F32),
                            pltpu.VMEM((n_pages, 8, LANES), F32)]),
        out_shape=[jax.ShapeDtypeStruct((n, 1, WIDTH_A), BF16), jax.ShapeDtypeStruct((n, 1, WIDTH_B), BF16)],
        compiler_params=_cparams("parallel"),
        name="even_decode",
    )(page_table, r3(qa_pad), r3(qi_pad), r3(wi), r3(qb_pad), r3(self_rows), *([cache_t] * n_pages))
    return oa.reshape(n, WIDTH_A), ob.reshape(n, WIDTH_B)


def _hgrn_lower_bound(lb_ref, layer):
    x = lb_ref[...]
    e = jnp.exp(x - jnp.max(x, axis=0, keepdims=True))
    sm = e / jnp.sum(e, axis=0, keepdims=True)
    lb = jnp.zeros((1, sm.shape[1]), F32)
    for r in range(1, layer + 1):
        lb = lb + sm[r:r + 1, :]
    return lb


def _hgrn_gates(qr, fl, lb):
    q = qr * jax.nn.sigmoid(qr) * HGRN_DK ** -0.5
    log_f = jnp.log(lb + (1.0 - lb) * jax.nn.sigmoid(fl))
    k = (1.0 - lb) * jax.nn.sigmoid(-fl)
    return q, k, log_f


def _hgrn_out(o, gr, nw):
    o = o * lax.rsqrt(jnp.mean(o * o, axis=-1, keepdims=True) + RMS_EPS) * nw
    return o * (gr * jax.nn.sigmoid(gr))


def _hgrn_prompt_kernel(q_ref, f_ref, i_ref, g_ref, lb_ref, nw_ref, o_ref, s_ref, st_sc, *, c, m, hp, layer):
    ci = pl.program_id(2)

    @pl.when(ci == 0)
    def _():
        st_sc[...] = jnp.zeros(st_sc.shape, F32)

    heads = range(hp)
    hs = lambda ref, h: ref[:, h * HGRN_DK:(h + 1) * HGRN_DK]
    lb_all = _hgrn_lower_bound(lb_ref, layer)
    gates = [_hgrn_gates(hs(q_ref, h), hs(f_ref, h), hs(lb_all, h)) for h in heads]
    q = [g[0] for g in gates]
    k = [g[1] for g in gates]
    v = [hs(i_ref, h) for h in heads]
    vb = [x.astype(BF16) for x in v]
    t_idx = lax.broadcasted_iota(jnp.int32, (c, c), 0)
    s_idx = lax.broadcasted_iota(jnp.int32, (c, c), 1)
    tril = (s_idx <= t_idx).astype(BF16)
    tril2 = jnp.concatenate([tril, tril], axis=1)
    G = [_dot(tril2, jnp.concatenate(_split_hi_lo(g[2]), axis=0)) for g in gates]

    st = [st_sc[h] for h in heads]
    o = [_dot_nt((q[h] * jnp.exp(G[h])).astype(BF16), st[h].astype(BF16)) for h in heads]

    pieces = [[jnp.zeros((m, HGRN_DV), F32)] for _ in heads]
    for blk in range(1, c // m):
        lo_r, hi_r = blk * m, (blk + 1) * m
        scs = []
        for h in heads:
            g_row = G[h][lo_r - 1:lo_r, :]
            qt = q[h][lo_r:hi_r] * jnp.exp(G[h][lo_r:hi_r] - g_row)
            kt = k[h][0:lo_r] * jnp.exp(g_row - G[h][0:lo_r])
            scs.append(_dot_nt(qt.astype(BF16), kt.astype(BF16)))
        for h in heads:
            pieces[h].append(_dot(scs[h].astype(BF16), vb[h][0:lo_r]))
    o = [o[h] + jnp.concatenate(pieces[h], axis=0) for h in heads]

    st_new = []
    for h in heads:
        g_end = G[h][c - 1:c, :]
        kd = k[h] * jnp.exp(g_end - G[h])
        st_new.append(st[h] * jnp.exp(g_end) + _dot_tn(vb[h], kd.astype(BF16)))
    for h in heads:
        st_sc[h] = st_new[h]

    in_blk = lax.broadcasted_iota(jnp.int32, (c, 1), 0) % m
    for d in range(m):
        for h in heads:
            ks = k[h] if d == 0 else pltpu.roll(k[h], d, 0)
            gs = G[h] if d == 0 else pltpu.roll(G[h], d, 0)
            vs = v[h] if d == 0 else pltpu.roll(v[h], d, 0)
            a = jnp.sum(q[h] * ks * jnp.exp(jnp.minimum(G[h] - gs, 0.0)), axis=1, keepdims=True)
            o[h] = o[h] + jnp.where(in_blk >= d, a, 0.0) * vs

    nw = nw_ref[...]
    for h in heads:
        o_ref[:, h * HGRN_DV:(h + 1) * HGRN_DV] = _hgrn_out(o[h], hs(g_ref, h), nw).astype(o_ref.dtype)

    @pl.when(ci == pl.num_programs(2) - 1)
    def _():
        for h in heads:
            s_ref[h] = st_new[h].T


def _hgrn_prompt(h4, lbs_raw, norm_w, batch, t, layer, c=128, m=16, hp=H_C):
    c = min(c, t)
    nc = t // c
    ng = H_C // hp
    w = hp * HGRN_DK
    col = lambda part: (lambda b, g, i: (b * nc + i, part * ng + g))
    return pl.pallas_call(
        functools.partial(_hgrn_prompt_kernel, c=c, m=m, hp=hp, layer=layer),
        grid=(batch, ng, nc),
        in_specs=[pl.BlockSpec((c, w), col(0)), pl.BlockSpec((c, w), col(1)),
                  pl.BlockSpec((c, w), col(2)), pl.BlockSpec((c, w), col(3)),
                  pl.BlockSpec((lbs_raw.shape[0], w), lambda b, g, i: (0, g)),
                  pl.BlockSpec((1, HGRN_DV), lambda b, g, i: (0, 0))],
        out_specs=[pl.BlockSpec((c, w), lambda b, g, i: (b * nc + i, g)),
                   pl.BlockSpec((None, hp, HGRN_DK, HGRN_DV), lambda b, g, i: (b, g, 0, 0))],
        out_shape=[jax.ShapeDtypeStruct((batch * t, D_MODEL), BF16),
                   jax.ShapeDtypeStruct((batch, H_C, HGRN_DK, HGRN_DV), F32)],
        scratch_shapes=[pltpu.VMEM((hp, HGRN_DV, HGRN_DK), F32)],
        compiler_params=_cparams("parallel", "parallel", "arbitrary"),
        name="hgrn_prompt",
    )(h4, h4, h4, h4, lbs_raw, norm_w.reshape(1, HGRN_DV))


def _lane_to_sublane(row):
    n = row.shape[1]
    eye = lax.broadcasted_iota(jnp.int32, (n, n), 0) == lax.broadcasted_iota(jnp.int32, (n, n), 1)
    return jnp.sum(jnp.where(eye, jnp.broadcast_to(row, (n, n)), 0.0), axis=1, keepdims=True)


def _hgrn_decode_kernel(h_ref, s0_ref, lb_ref, nw_ref, o_ref, s_ref, *, layer):
    lb_all = _hgrn_lower_bound(lb_ref, layer)
    nw = nw_ref[...]
    for h in range(H_C):
        sl = lambda part: slice((part * H_C + h) * HGRN_DK, (part * H_C + h + 1) * HGRN_DK)
        q, k, log_f = _hgrn_gates(h_ref[:, sl(0)], h_ref[:, sl(1)], lb_all[:, h * HGRN_DK:(h + 1) * HGRN_DK])
        v = h_ref[:, sl(2)]
        s_new = _lane_to_sublane(jnp.exp(log_f)) * s0_ref[h] + _lane_to_sublane(k) * v
        s_ref[h] = s_new
        o = jnp.sum(_lane_to_sublane(q) * s_new, axis=0, keepdims=True)
        o_ref[:, h * HGRN_DV:(h + 1) * HGRN_DV] = _hgrn_out(o, h_ref[:, sl(3)], nw).astype(o_ref.dtype)


def _hgrn_decode(h4, s0, lbs_raw, norm_w, layer):
    n = h4.shape[0]
    state_spec = pl.BlockSpec((None, H_C, HGRN_DK, HGRN_DV), lambda b: (b, 0, 0, 0))
    o, s = pl.pallas_call(
        functools.partial(_hgrn_decode_kernel, layer=layer),
        grid=(n,),
        in_specs=[pl.BlockSpec((None, 1, 4 * D_MODEL), lambda b: (b, 0, 0)), state_spec,
                  pl.BlockSpec(lbs_raw.shape, lambda b: (0, 0)), pl.BlockSpec((1, HGRN_DV), lambda b: (0, 0))],
        out_specs=[pl.BlockSpec((None, 1, D_MODEL), lambda b: (b, 0, 0)), state_spec],
        out_shape=[jax.ShapeDtypeStruct((n, 1, D_MODEL), BF16), jax.ShapeDtypeStruct(s0.shape, F32)],
        compiler_params=_cparams("parallel"),
        name="hgrn_decode",
    )(h4.reshape(n, 1, 4 * D_MODEL), s0, lbs_raw, norm_w.reshape(1, HGRN_DV))
    return o.reshape(n, D_MODEL), s


MOE_TILE = 512


def _router_kernel(x_ref, w_ref, o_ref):
    logits = _dot(x_ref[...].astype(BF16), w_ref[...])
    lane = lax.broadcasted_iota(jnp.int32, logits.shape, 1)
    lg = jnp.where(lane < N_EXPERTS, logits, -jnp.inf)
    m1 = jnp.max(lg, axis=1, keepdims=True)
    i1 = jnp.min(jnp.where(lg == m1, lane, LANES), axis=1, keepdims=True)
    lg2 = jnp.where(lane == i1, -jnp.inf, lg)
    m2 = jnp.max(lg2, axis=1, keepdims=True)
    i2 = jnp.min(jnp.where(lg2 == m2, lane, LANES), axis=1, keepdims=True)
    e2 = jnp.exp(m2 - m1)
    g1 = 1.0 / (1.0 + e2)
    out = jnp.where(lane == 0, g1, jnp.where(lane == 1, e2 * g1, jnp.where(
        lane == 2, i1.astype(F32), jnp.where(lane == 3, i2.astype(F32), 0.0))))
    o_ref[...] = out


def _router(x, w_router_pad, tm=TOKEN_TILE):
    n = x.shape[0]
    return pl.pallas_call(
        _router_kernel,
        grid=(pl.cdiv(n, tm),),
        in_specs=[pl.BlockSpec((tm, D_MODEL), lambda i: (i, 0)), pl.BlockSpec((D_MODEL, LANES), lambda i: (0, 0))],
        out_specs=pl.BlockSpec((tm, LANES), lambda i: (i, 0)),
        out_shape=jax.ShapeDtypeStruct((n, LANES), F32),
        compiler_params=_cparams("parallel"),
        name="router",
    )(x, w_router_pad)


def _moe_in_kernel(te_ref, nu_ref, xs_ref, wa_ref, wb_ref, o_ref):
    used = pl.program_id(0) < nu_ref[0]

    @pl.when(used)
    def _():
        xb = xs_ref[...]
        a = _dot(xb, wa_ref[...])
        b = _dot(xb, wb_ref[...])
        o_ref[...] = (a * jax.nn.sigmoid(a) * b).astype(o_ref.dtype)

    @pl.when(jnp.logical_not(used))
    def _():
        o_ref[...] = jnp.zeros(o_ref.shape, o_ref.dtype)


def _moe_out_kernel(te_ref, nu_ref, h_ref, w_ref, o_ref):
    used = pl.program_id(0) < nu_ref[0]

    @pl.when(used)
    def _():
        o_ref[...] = _dot(h_ref[...], w_ref[...])

    @pl.when(jnp.logical_not(used))
    def _():
        o_ref[...] = jnp.zeros(o_ref.shape, o_ref.dtype)


def _moe_experts(xs, tile_expert, n_used, w_in, w_out, tn=D_FF_EXPERT // 2):
    p = xs.shape[0]
    f = w_out.shape[1]
    tn = min(tn, f)
    nj = f // tn
    nt = p // MOE_TILE
    h = pl.pallas_call(
        _moe_in_kernel,
        grid_spec=pltpu.PrefetchScalarGridSpec(
            num_scalar_prefetch=2, grid=(nt, nj),
            in_specs=[pl.BlockSpec((MOE_TILE, D_MODEL), lambda i, j, te, nu: (i, 0)),
                      pl.BlockSpec((None, D_MODEL, tn), lambda i, j, te, nu: (te[i], 0, j)),
                      pl.BlockSpec((None, D_MODEL, tn), lambda i, j, te, nu: (te[i], 0, j + nj))],
            out_specs=pl.BlockSpec((MOE_TILE, tn), lambda i, j, te, nu: (i, j))),
        out_shape=jax.ShapeDtypeStruct((p, f), BF16),
        compiler_params=_cparams("parallel", "arbitrary"),
        name="moe_in",
    )(tile_expert, n_used, xs, w_in, w_in)
    return pl.pallas_call(
        _moe_out_kernel,
        grid_spec=pltpu.PrefetchScalarGridSpec(
            num_scalar_prefetch=2, grid=(nt,),
            in_specs=[pl.BlockSpec((MOE_TILE, f), lambda i, te, nu: (i, 0)),
                      pl.BlockSpec((None, f, D_MODEL), lambda i, te, nu: (te[i], 0, 0))],
            out_specs=pl.BlockSpec((MOE_TILE, D_MODEL), lambda i, te, nu: (i, 0))),
        out_shape=jax.ShapeDtypeStruct((p, D_MODEL), F32),
        compiler_params=_cparams("parallel"),
        name="moe_out",
    )(tile_expert, n_used, h, w_out)


def _moe_plan(route):
    n = route.shape[0]
    e_idx = route[:, 2:4].astype(jnp.int32).reshape(-1)
    onehot = (e_idx[:, None] == jnp.arange(N_EXPERTS, dtype=jnp.int32)[None, :]).astype(jnp.int32)
    running = jnp.cumsum(onehot, axis=0)
    rank = jnp.take_along_axis(running, e_idx[:, None], axis=1)[:, 0] - 1
    counts = running[-1]
    tiles = (counts + MOE_TILE - 1) // MOE_TILE
    tile_end = jnp.cumsum(tiles)
    tile_start = tile_end - tiles
    nt = (2 * n + N_EXPERTS * (MOE_TILE - 1) + MOE_TILE - 1) // MOE_TILE
    dest = tile_start[e_idx] * MOE_TILE + rank
    token = jnp.zeros((nt * MOE_TILE,), jnp.int32).at[dest].set(jnp.arange(2 * n, dtype=jnp.int32) // 2)
    past_end = (jnp.arange(nt, dtype=jnp.int32)[:, None] >= tile_end[None, :]).astype(jnp.int32)
    tile_expert = jnp.minimum(jnp.sum(past_end, axis=1), N_EXPERTS - 1).astype(jnp.int32)
    return token, dest.reshape(n, 2), tile_expert, tile_end[-1:].astype(jnp.int32)


def _moe(x1, w_router_pad, w_in, w_out):
    route = _router(x1, w_router_pad)
    token, pos, tile_expert, n_used = _moe_plan(route)
    xs = x1.astype(BF16)[token]
    ys = _moe_experts(xs, tile_expert, n_used, w_in, w_out)
    return ys[pos[:, 0]], ys[pos[:, 1]], route


def kernel(x_prompt, x_sample, cache_attn, state_hgrn, page_table, p_prompt, p_sample, w_in_even, w_out_even,
           w_in_odd, w_out_odd, hgrn_lower_bounds, hgrn_norm_w, ln_mix_g, ln_mix_b, ln_ffn_g, ln_ffn_b,
           w_ffn_in, w_ffn_out, w_router, w_moe_in, w_moe_out, w_ple_proj, w_ple_gate):
    batch, t, d = x_prompt.shape
    n_seq = x_sample.shape[0]
    n = batch * t
    past_len = page_table.shape[1] * PAGE_SIZE
    x = jnp.concatenate([x_prompt.reshape(n, d), x_sample.reshape(n_seq, d)], axis=0)
    pos = jnp.concatenate([jnp.tile(jnp.arange(t, dtype=jnp.int32), batch), jnp.full((n_seq,), past_len, jnp.int32)])
    cs = _rotary_table(pos)
    p_all = jnp.concatenate([p_prompt.reshape(DEPTH, n, PLE_DIM), p_sample.reshape(DEPTH, n_seq, PLE_DIM)], axis=1)
    join = lambda a, b: jnp.concatenate([a, b], axis=0)
    cache_t = jnp.swapaxes(cache_attn, 2, 3)

    rows_p, rows_s, state_p, state_s = [], [], [], []
    for i in range(DEPTH):
        j = i // 2
        ple = (p_all[i], w_ple_gate[i].astype(BF16), w_ple_proj[i].astype(BF16))
        if i % 2 == 0:
            w_rows, w_q = _prep_even_in(w_in_even[j])
            rows, qa, qi, wi, qb, ka, va, ki, kb, vb = _even_inproj(x, w_rows, w_q, cs)
            oa_p = _dsa_prompt(qa, qi, wi, ka, va, ki, batch, t)
            ob_p = _sb_prompt(qb, kb, vb, batch, t)
            oa_s, ob_s = _even_decode(page_table, cache_t, j, qa[n:], qi[n:], wi[n:], qb[n:], rows[n:])
            w_o = w_out_even[j].astype(BF16)
            x1 = _mix_ln(x, ln_mix_g[i], ln_mix_b[i],
                         mm=[(join(oa_p, oa_s), w_o[:WIDTH_A]), (join(ob_p, ob_s), w_o[WIDTH_A:])])
            h = _swiglu_in(x1, w_ffn_in[j].astype(BF16))
            x = _mix_ln(x1, ln_ffn_g[i], ln_ffn_b[i], mm=[(h, w_ffn_out[j].astype(BF16))], ple=ple)
            rows_p.append(rows[:n].reshape(batch, t, ROW_DIM))
            rows_s.append(rows[n:].reshape(n_seq, 1, ROW_DIM))
        else:
            h4 = _matmul(x, w_in_odd[j].astype(BF16))
            o_p, s_p = _hgrn_prompt(h4, hgrn_lower_bounds, hgrn_norm_w[j], batch, t, j)
            o_s, s_s = _hgrn_decode(h4[n:], state_hgrn[j], hgrn_lower_bounds, hgrn_norm_w[j], j)
            x1 = _mix_ln(x, ln_mix_g[i], ln_mix_b[i], mm=[(join(o_p, o_s), w_out_odd[j].astype(BF16))])
            w_r = jnp.pad(w_router[j], ((0, 0), (0, LANES - N_EXPERTS))).astype(BF16)
            y1, y2, route = _moe(x1, w_r, w_moe_in[j].astype(BF16), w_moe_out[j].astype(BF16))
            x = _mix_ln(x1, ln_ffn_g[i], ln_ffn_b[i], comb=(y1, y2), gates=route, ple=ple)
            state_p.append(s_p)
            state_s.append(s_s)
    return (x[:n].reshape(batch, t, d), x[n:].reshape(n_seq, 1, d), jnp.stack(rows_p), jnp.stack(rows_s),
            jnp.stack(state_p), jnp.stack(state_s))
```
